```python
import jax, jax.numpy as jnp
from jax import lax
import numpy as np

D_MODEL = 1024
BATCH = 8
SEQ = 4096
DEPTH = 4

HEAD_DIM = D_MODEL // 16
SB_HEADS = 6
ML_HEADS = 6
SG_GROUPS = 4
SB_WIDTH = SB_HEADS * HEAD_DIM
ML_WIDTH = ML_HEADS * HEAD_DIM
SG_WIDTH = SG_GROUPS * HEAD_DIM
MIX_HEADS = SB_HEADS + ML_HEADS + SG_GROUPS
MIX_WIDTH = SB_WIDTH + ML_WIDTH + SG_WIDTH
IN_SIZES = (SB_WIDTH, SB_WIDTH, SB_WIDTH, 2 * ML_WIDTH, ML_WIDTH, ML_WIDTH, ML_HEADS, ML_HEADS, 2 * SG_WIDTH)
IN_WIDTH = sum(IN_SIZES)
Q_BLOCK = 128
ML_CHUNK = 128
SG_CHUNK = 128
CONV_WIDTH = 4
PEER_HEADS = 8
PEER_NKEYS = 128
PEER_EXPERTS = PEER_NKEYS * PEER_NKEYS
PEER_SUBDIM = 128
PEER_TOPK = 16
PEER_BLOCK = 128
NORM_EPS = 1e-6

kernel_name = "hybrid_sb_mlstm_sgmlp_peer"


def rms_norm(x, g):
    x32 = x.astype(jnp.float32)
    return x32 * lax.rsqrt(jnp.mean(x32 * x32, axis=-1, keepdims=True) + NORM_EPS) * g.astype(jnp.float32)


def split_heads(t, n_heads):
    b, s, _ = t.shape
    return t.reshape(b, s, n_heads, HEAD_DIM).transpose(0, 2, 1, 3)


def causal_conv(t, w, b):
    s = t.shape[1]
    tp = jnp.pad(t, ((0, 0), (CONV_WIDTH - 1, 0), (0, 0)))
    out = b.astype(jnp.float32)
    for j in range(CONV_WIDTH):
        out = out + tp[:, j:j + s] * w[j]
    return out


def stick_breaking_attention(q, k, v):
    s_len = q.shape[2]
    scale = HEAD_DIM ** -0.5
    outs = []
    for blk in range(s_len // Q_BLOCK):
        start = blk * Q_BLOCK
        end = start + Q_BLOCK
        z = jnp.einsum('bhqd,bhkd->bhqk', q[:, :, start:end], k[:, :, :end]) * scale
        before = jnp.arange(end)[None, :] < (start + jnp.arange(Q_BLOCK))[:, None]
        log_keep = jnp.where(before, -jax.nn.softplus(z), 0.0)
        log_keep_after = lax.cumsum(log_keep, axis=3, reverse=True) - log_keep
        w = jnp.where(before, jnp.exp(jax.nn.log_sigmoid(z) + log_keep_after), 0.0)
        outs.append(jnp.einsum('bhqk,bhkd->bhqd', w, v[:, :, :end]))
    return jnp.concatenate(outs, axis=2)


def mlstm_chunkwise(q, k, v, i_pre, f_pre):
    b, h, s_len, d = q.shape
    nc = s_len // ML_CHUNK
    log_f = jax.nn.log_sigmoid(f_pre)

    def chunks(t):
        return jnp.moveaxis(t.reshape(b, h, nc, ML_CHUNK, *t.shape[3:]), 2, 0)

    tril = jnp.tril(jnp.ones((ML_CHUNK, ML_CHUNK), dtype=bool))

    def step(carry, inp):
        c_mat, n_vec, m = carry
        qc, kc, vc, ic, lfc = inp
        bcum = jnp.cumsum(lfc, axis=-1)
        dmat = jnp.where(tril, bcum[..., :, None] - bcum[..., None, :] + ic[..., None, :], -jnp.inf)
        m_inter = bcum + m[..., None]
        m_t = jnp.maximum(m_inter, jnp.max(dmat, axis=-1))
        inter_scale = jnp.exp(m_inter - m_t)
        qk = jnp.einsum('bhtd,bhsd->bhts', qc, kc) * jnp.exp(dmat - m_t[..., None])
        num = jnp.einsum('bhts,bhsd->bhtd', qk, vc) + inter_scale[..., None] * jnp.einsum('bhtd,bhde->bhte', qc, c_mat)
        den = jnp.sum(qk, axis=-1) + inter_scale * jnp.einsum('bhtd,bhd->bht', qc, n_vec)
        h_out = num / jnp.maximum(jnp.abs(den), jnp.exp(-m_t))[..., None]
        b_last = bcum[..., -1]
        g_s = b_last[..., None] - bcum + ic
        m_new = jnp.maximum(b_last + m, jnp.max(g_s, axis=-1))
        w_s = jnp.exp(g_s - m_new[..., None])
        decay = jnp.exp(b_last + m - m_new)
        c_new = decay[..., None, None] * c_mat + jnp.einsum('bhs,bhsd,bhse->bhde', w_s, kc, vc)
        n_new = decay[..., None] * n_vec + jnp.einsum('bhs,bhsd->bhd', w_s, kc)
        return (c_new, n_new, m_new), h_out

    init = (jnp.zeros((b, h, d, d), jnp.float32), jnp.zeros((b, h, d), jnp.float32), jnp.zeros((b, h), jnp.float32))
    _, hs = lax.scan(step, init, (chunks(q), chunks(k), chunks(v), chunks(i_pre), chunks(log_f)))
    return jnp.moveaxis(hs, 0, 2).reshape(b, h, s_len, d)


def spatial_gating(u, v, w_s, b_s):
    b, s_len, _ = u.shape
    nc = s_len // SG_CHUNK
    tril = jnp.tril(jnp.ones((SG_CHUNK, SG_CHUNK), dtype=jnp.float32))
    v5 = v.reshape(b, nc, SG_CHUNK, SG_GROUPS, HEAD_DIM)
    gate = jnp.einsum('gts,bcsgd->bctgd', w_s.astype(jnp.float32) * tril, v5) + b_s.astype(jnp.float32).T[:, :, None]
    return (u.reshape(b, nc, SG_CHUNK, SG_GROUPS, HEAD_DIM) * gate).reshape(b, s_len, SG_GROUPS, HEAD_DIM)


def mixing_sublayer(xn, w_in, sb_qn_g, sb_kn_g, ml_conv_w, ml_conv_b, ml_i_b, ml_f_b,
                    sg_vn_g, sg_w, sg_b, out_g, w_out):
    b, s_len, _ = xn.shape
    z = jnp.matmul(xn, w_in).astype(jnp.float32)
    cuts = [int(c) for c in np.cumsum(IN_SIZES)[:-1]]
    sb_q, sb_k, sb_v, ml_qk, ml_v, ml_o, ml_i, ml_f, sg_uv = jnp.split(z, cuts, axis=-1)
    head_g = out_g.astype(jnp.float32).reshape(MIX_HEADS, HEAD_DIM)

    q = rms_norm(split_heads(sb_q, SB_HEADS), sb_qn_g)
    k = rms_norm(split_heads(sb_k, SB_HEADS), sb_kn_g)
    y_sb = stick_breaking_attention(q, k, split_heads(sb_v, SB_HEADS)).transpose(0, 2, 1, 3)
    y_sb = rms_norm(y_sb, head_g[:SB_HEADS])

    qk = jax.nn.silu(causal_conv(ml_qk, ml_conv_w.astype(jnp.float32), ml_conv_b))
    mq = split_heads(qk[..., :ML_WIDTH], ML_HEADS)
    mk = split_heads(qk[..., ML_WIDTH:], ML_HEADS) * (HEAD_DIM ** -0.5)
    i_pre = (ml_i + ml_i_b.astype(jnp.float32)).transpose(0, 2, 1)
    f_pre = (ml_f + ml_f_b.astype(jnp.float32)).transpose(0, 2, 1)
    y_ml = mlstm_chunkwise(mq, mk, split_heads(ml_v, ML_HEADS), i_pre, f_pre).transpose(0, 2, 1, 3)
    o_gate = jax.nn.sigmoid(ml_o).reshape(b, s_len, ML_HEADS, HEAD_DIM)
    y_ml = rms_norm(y_ml, head_g[SB_HEADS:SB_HEADS + ML_HEADS]) * o_gate

    uv = jax.nn.gelu(sg_uv)
    u = uv[..., :SG_WIDTH]
    v = rms_norm(uv[..., SG_WIDTH:].reshape(b, s_len, SG_GROUPS, HEAD_DIM),
                 sg_vn_g.reshape(SG_GROUPS, HEAD_DIM)).reshape(b, s_len, SG_WIDTH)
    y_sg = rms_norm(spatial_gating(u, v, sg_w, sg_b), head_g[SB_HEADS + ML_HEADS:])

    y = jnp.concatenate([y_sb, y_ml, y_sg], axis=2).reshape(b, s_len, MIX_WIDTH)
    return jnp.matmul(y.astype(xn.dtype), w_out)


def peer_ffn(xn, wq, sub_keys, u_tab, v_tab):
    b, s_len, d = xn.shape
    t = b * s_len
    xt = xn.reshape(t, d)
    q = jnp.matmul(xt, wq).astype(jnp.float32).reshape(t, PEER_HEADS, 2, PEER_SUBDIM)
    scores = jnp.einsum('thcd,hcnd->thcn', q, sub_keys.astype(jnp.float32))
    sv, si = lax.top_k(scores, PEER_TOPK)
    cand_s = (sv[:, :, 0, :, None] + sv[:, :, 1, None, :]).reshape(t, PEER_HEADS, PEER_TOPK * PEER_TOPK)
    cand_i = (si[:, :, 0, :, None] * PEER_NKEYS + si[:, :, 1, None, :]).reshape(t, PEER_HEADS, PEER_TOPK * PEER_TOPK)
    best_s, best_p = lax.top_k(cand_s, PEER_TOPK)
    idx = jnp.take_along_axis(cand_i, best_p, axis=-1)
    gate = jax.nn.softmax(best_s, axis=-1)
    nb = t // PEER_BLOCK

    def expert_block(args):
        xb, ib, gb = args
        act = jnp.einsum('td,thkd->thk', xb.astype(jnp.float32), u_tab[ib].astype(jnp.float32))
        wgt = jax.nn.gelu(act) * gb
        return jnp.einsum('thk,thkd->td', wgt, v_tab[ib].astype(jnp.float32))

    y = lax.map(expert_block, (xt.reshape(nb, PEER_BLOCK, d),
                               idx.reshape(nb, PEER_BLOCK, PEER_HEADS, PEER_TOPK),
                               gate.reshape(nb, PEER_BLOCK, PEER_HEADS, PEER_TOPK)))
    return y.reshape(b, s_len, d)


def setup_inputs(seed: int = 0) -> dict:
    key = jax.random.key(seed)
    ks = jax.random.split(key, 20)
    f32 = jnp.float32
    nrm = lambda k, shape, scale: jax.random.normal(k, shape, f32) * scale
    return {
        "x": nrm(ks[0], (BATCH, SEQ, D_MODEL), 1.0),
        "norm1_g": 1.0 + nrm(ks[1], (DEPTH, D_MODEL), 0.02),
        "w_in": nrm(ks[2], (DEPTH, D_MODEL, IN_WIDTH), D_MODEL ** -0.5),
        "sb_qn_g": 1.0 + nrm(ks[3], (DEPTH, HEAD_DIM), 0.02),
        "sb_kn_g": 1.0 + nrm(ks[4], (DEPTH, HEAD_DIM), 0.02),
        "ml_conv_w": nrm(ks[5], (DEPTH, CONV_WIDTH, 2 * ML_WIDTH), CONV_WIDTH ** -0.5),
        "ml_conv_b": nrm(ks[6], (DEPTH, 2 * ML_WIDTH), 0.02),
        "ml_i_b": nrm(ks[7], (DEPTH, ML_HEADS), 0.1),
        "ml_f_b": jnp.linspace(3.0, 6.0, ML_HEADS, dtype=f32)[None, :] + nrm(ks[8], (DEPTH, ML_HEADS), 0.02),
        "sg_vn_g": 1.0 + nrm(ks[9], (DEPTH, SG_WIDTH), 0.02),
        "sg_w": nrm(ks[10], (DEPTH, SG_GROUPS, SG_CHUNK, SG_CHUNK), SG_CHUNK ** -0.5),
        "sg_b": 1.0 + nrm(ks[11], (DEPTH, SG_GROUPS, SG_CHUNK), 0.02),
        "out_g": 1.0 + nrm(ks[12], (DEPTH, MIX_WIDTH), 0.02),
        "w_out": nrm(ks[13], (DEPTH, MIX_WIDTH, D_MODEL), MIX_WIDTH ** -0.5),
        "norm2_g": 1.0 + nrm(ks[14], (DEPTH, D_MODEL), 0.02),
        "peer_wq": nrm(ks[15], (DEPTH, D_MODEL, PEER_HEADS * 2 * PEER_SUBDIM), D_MODEL ** -0.5),
        "peer_keys": nrm(ks[16], (DEPTH, PEER_HEADS, 2, PEER_NKEYS, PEER_SUBDIM), PEER_SUBDIM ** -0.5),
        "peer_u": nrm(ks[17], (DEPTH, PEER_EXPERTS, D_MODEL), D_MODEL ** -0.5),
        "peer_v": nrm(ks[18], (DEPTH, PEER_EXPERTS, D_MODEL), (PEER_HEADS * PEER_TOPK) ** -0.5),
    }


def reference(x, norm1_g, w_in, sb_qn_g, sb_kn_g, ml_conv_w, ml_conv_b, ml_i_b, ml_f_b,
              sg_vn_g, sg_w, sg_b, out_g, w_out, norm2_g, peer_wq, peer_keys, peer_u, peer_v):
    for l in range(DEPTH):
        xn = rms_norm(x, norm1_g[l]).astype(x.dtype)
        x = x + mixing_sublayer(xn, w_in[l], sb_qn_g[l], sb_kn_g[l], ml_conv_w[l], ml_conv_b[l],
                                ml_i_b[l], ml_f_b[l], sg_vn_g[l], sg_w[l], sg_b[l], out_g[l],
                                w_out[l]).astype(x.dtype)
        xn = rms_norm(x, norm2_g[l]).astype(x.dtype)
        x = x + peer_ffn(xn, peer_wq[l], peer_keys[l], peer_u[l], peer_v[l]).astype(x.dtype)
    return x
```

```python
import functools

import jax
import jax.numpy as jnp
import numpy as np
from jax import lax
from jax.experimental import pallas as pl
from jax.experimental.pallas import tpu as pltpu

D_MODEL = 1024
DEPTH = 4
HEAD_DIM = 64
SB_HEADS = 6
ML_HEADS = 6
SG_GROUPS = 4
SB_WIDTH = SB_HEADS * HEAD_DIM
ML_WIDTH = ML_HEADS * HEAD_DIM
SG_WIDTH = SG_GROUPS * HEAD_DIM
MIX_HEADS = SB_HEADS + ML_HEADS + SG_GROUPS
MIX_WIDTH = SB_WIDTH + ML_WIDTH + SG_WIDTH
IN_SIZES = (SB_WIDTH, SB_WIDTH, SB_WIDTH, 2 * ML_WIDTH, ML_WIDTH, ML_WIDTH, ML_HEADS, ML_HEADS, 2 * SG_WIDTH)
IN_WIDTH = sum(IN_SIZES)
Q_BLOCK = 128
ML_CHUNK = 128
SG_CHUNK = 128
CONV_WIDTH = 4
PEER_HEADS = 8
PEER_NKEYS = 128
PEER_EXPERTS = PEER_NKEYS * PEER_NKEYS
PEER_SUBDIM = 128
PEER_TOPK = 16
PEER_BLOCK = 128
NORM_EPS = 1e-6

VMEM_LIMIT_BYTES = 56 * 1024 * 1024


def _norm_matmul_body(x_ref, g_ref, w_ref, o_ref):
    x = x_ref[...]
    xn = x * lax.rsqrt(jnp.mean(x * x, axis=-1, keepdims=True) + NORM_EPS) * g_ref[...]
    o_ref[...] = jnp.dot(xn.astype(jnp.bfloat16), w_ref[...], preferred_element_type=jnp.float32)


def norm_matmul(x, g, w, tm=512):
    t, d = x.shape
    n = w.shape[1]
    return pl.pallas_call(
        _norm_matmul_body,
        grid=(t // tm,),
        in_specs=[
            pl.BlockSpec((tm, d), lambda i: (i, 0)),
            pl.BlockSpec((1, d), lambda i: (0, 0)),
            pl.BlockSpec((d, n), lambda i: (0, 0)),
        ],
        out_specs=pl.BlockSpec((tm, n), lambda i: (i, 0)),
        out_shape=jax.ShapeDtypeStruct((t, n), jnp.float32),
        compiler_params=pltpu.CompilerParams(
            dimension_semantics=("parallel",), vmem_limit_bytes=VMEM_LIMIT_BYTES),
        name="norm_matmul",
    )(x, g.reshape(1, d), w)


def _rms_norm(x, g):
    x32 = x.astype(jnp.float32)
    return x32 * lax.rsqrt(jnp.mean(x32 * x32, axis=-1, keepdims=True) + NORM_EPS) * g.astype(jnp.float32)


def _split_heads(t, n_heads):
    b, s, _ = t.shape
    return t.reshape(b, s, n_heads, HEAD_DIM).transpose(0, 2, 1, 3)


def _causal_conv(t, w, b):
    s = t.shape[1]
    tp = jnp.pad(t, ((0, 0), (CONV_WIDTH - 1, 0), (0, 0)))
    out = b.astype(jnp.float32)
    for j in range(CONV_WIDTH):
        out = out + tp[:, j:j + s] * w[j]
    return out


def _stick_breaking_attention(q, k, v):
    s_len = q.shape[2]
    scale = HEAD_DIM ** -0.5
    outs = []
    for blk in range(s_len // Q_BLOCK):
        start = blk * Q_BLOCK
        end = start + Q_BLOCK
        z = jnp.einsum('bhqd,bhkd->bhqk', q[:, :, start:end], k[:, :, :end]) * scale
        before = jnp.arange(end)[None, :] < (start + jnp.arange(Q_BLOCK))[:, None]
        log_keep = jnp.where(before, -jax.nn.softplus(z), 0.0)
        log_keep_after = lax.cumsum(log_keep, axis=3, reverse=True) - log_keep
        w = jnp.where(before, jnp.exp(jax.nn.log_sigmoid(z) + log_keep_after), 0.0)
        outs.append(jnp.einsum('bhqk,bhkd->bhqd', w, v[:, :, :end]))
    return jnp.concatenate(outs, axis=2)


def _mlstm_chunkwise(q, k, v, i_pre, f_pre):
    b, h, s_len, d = q.shape
    nc = s_len // ML_CHUNK
    log_f = jax.nn.log_sigmoid(f_pre)

    def chunks(t):
        return jnp.moveaxis(t.reshape(b, h, nc, ML_CHUNK, *t.shape[3:]), 2, 0)

    tril = jnp.tril(jnp.ones((ML_CHUNK, ML_CHUNK), dtype=bool))

    def step(carry, inp):
        c_mat, n_vec, m = carry
        qc, kc, vc, ic, lfc = inp
        bcum = jnp.cumsum(lfc, axis=-1)
        dmat = jnp.where(tril, bcum[..., :, None] - bcum[..., None, :] + ic[..., None, :], -jnp.inf)
        m_inter = bcum + m[..., None]
        m_t = jnp.maximum(m_inter, jnp.max(dmat, axis=-1))
        inter_scale = jnp.exp(m_inter - m_t)
        qk = jnp.einsum('bhtd,bhsd->bhts', qc, kc) * jnp.exp(dmat - m_t[..., None])
        num = jnp.einsum('bhts,bhsd->bhtd', qk, vc) + inter_scale[..., None] * jnp.einsum('bhtd,bhde->bhte', qc, c_mat)
        den = jnp.sum(qk, axis=-1) + inter_scale * jnp.einsum('bhtd,bhd->bht', qc, n_vec)
        h_out = num / jnp.maximum(jnp.abs(den), jnp.exp(-m_t))[..., None]
        b_last = bcum[..., -1]
        g_s = b_last[..., None] - bcum + ic
        m_new = jnp.maximum(b_last + m, jnp.max(g_s, axis=-1))
        w_s = jnp.exp(g_s - m_new[..., None])
        decay = jnp.exp(b_last + m - m_new)
        c_new = decay[..., None, None] * c_mat + jnp.einsum('bhs,bhsd,bhse->bhde', w_s, kc, vc)
        n_new = decay[..., None] * n_vec + jnp.einsum('bhs,bhsd->bhd', w_s, kc)
        return (c_new, n_new, m_new), h_out

    init = (jnp.zeros((b, h, d, d), jnp.float32), jnp.zeros((b, h, d), jnp.float32), jnp.zeros((b, h), jnp.float32))
    _, hs = lax.scan(step, init, (chunks(q), chunks(k), chunks(v), chunks(i_pre), chunks(log_f)))
    return jnp.moveaxis(hs, 0, 2).reshape(b, h, s_len, d)


def _spatial_gating(u, v, w_s, b_s):
    b, s_len, _ = u.shape
    nc = s_len // SG_CHUNK
    tril = jnp.tril(jnp.ones((SG_CHUNK, SG_CHUNK), dtype=jnp.float32))
    v5 = v.reshape(b, nc, SG_CHUNK, SG_GROUPS, HEAD_DIM)
    gate = jnp.einsum('gts,bcsgd->bctgd', w_s.astype(jnp.float32) * tril, v5) + b_s.astype(jnp.float32).T[:, :, None]
    return (u.reshape(b, nc, SG_CHUNK, SG_GROUPS, HEAD_DIM) * gate).reshape(b, s_len, SG_GROUPS, HEAD_DIM)


def _mixing_sublayer(x, norm_g, w_in, sb_qn_g, sb_kn_g, ml_conv_w, ml_conv_b, ml_i_b, ml_f_b,
                     sg_vn_g, sg_w, sg_b, out_g, w_out):
    b, s_len, d = x.shape
    n_pad = -IN_WIDTH % 128
    w_pad = jnp.pad(w_in, ((0, 0), (0, n_pad))).astype(jnp.bfloat16)
    z = norm_matmul(x.reshape(b * s_len, d), norm_g, w_pad)[:, :IN_WIDTH].reshape(b, s_len, IN_WIDTH)
    cuts = [int(c) for c in np.cumsum(IN_SIZES)[:-1]]
    sb_q, sb_k, sb_v, ml_qk, ml_v, ml_o, ml_i, ml_f, sg_uv = jnp.split(z, cuts, axis=-1)
    head_g = out_g.astype(jnp.float32).reshape(MIX_HEADS, HEAD_DIM)

    q = _rms_norm(_split_heads(sb_q, SB_HEADS), sb_qn_g)
    k = _rms_norm(_split_heads(sb_k, SB_HEADS), sb_kn_g)
    y_sb = _stick_breaking_attention(q, k, _split_heads(sb_v, SB_HEADS)).transpose(0, 2, 1, 3)
    y_sb = _rms_norm(y_sb, head_g[:SB_HEADS])

    qk = jax.nn.silu(_causal_conv(ml_qk, ml_conv_w.astype(jnp.float32), ml_conv_b))
    mq = _split_heads(qk[..., :ML_WIDTH], ML_HEADS)
    mk = _split_heads(qk[..., ML_WIDTH:], ML_HEADS) * (HEAD_DIM ** -0.5)
    i_pre = (ml_i + ml_i_b.astype(jnp.float32)).transpose(0, 2, 1)
    f_pre = (ml_f + ml_f_b.astype(jnp.float32)).transpose(0, 2, 1)
    y_ml = _mlstm_chunkwise(mq, mk, _split_heads(ml_v, ML_HEADS), i_pre, f_pre).transpose(0, 2, 1, 3)
    o_gate = jax.nn.sigmoid(ml_o).reshape(b, s_len, ML_HEADS, HEAD_DIM)
    y_ml = _rms_norm(y_ml, head_g[SB_HEADS:SB_HEADS + ML_HEADS]) * o_gate

    uv = jax.nn.gelu(sg_uv)
    u = uv[..., :SG_WIDTH]
    v = _rms_norm(uv[..., SG_WIDTH:].reshape(b, s_len, SG_GROUPS, HEAD_DIM),
                  sg_vn_g.reshape(SG_GROUPS, HEAD_DIM)).reshape(b, s_len, SG_WIDTH)
    y_sg = _rms_norm(_spatial_gating(u, v, sg_w, sg_b), head_g[SB_HEADS + ML_HEADS:])

    y = jnp.concatenate([y_sb, y_ml, y_sg], axis=2).reshape(b, s_len, MIX_WIDTH)
    return jnp.matmul(y, w_out)


def _peer_ffn(xn, wq, sub_keys, u_tab, v_tab):
    b, s_len, d = xn.shape
    t = b * s_len
    xt = xn.reshape(t, d)
    q = jnp.matmul(xt, wq).astype(jnp.float32).reshape(t, PEER_HEADS, 2, PEER_SUBDIM)
    scores = jnp.einsum('thcd,hcnd->thcn', q, sub_keys.astype(jnp.float32))
    sv, si = lax.top_k(scores, PEER_TOPK)
    cand_s = (sv[:, :, 0, :, None] + sv[:, :, 1, None, :]).reshape(t, PEER_HEADS, PEER_TOPK * PEER_TOPK)
    cand_i = (si[:, :, 0, :, None] * PEER_NKEYS + si[:, :, 1, None, :]).reshape(t, PEER_HEADS, PEER_TOPK * PEER_TOPK)
    best_s, best_p = lax.top_k(cand_s, PEER_TOPK)
    idx = jnp.take_along_axis(cand_i, best_p, axis=-1)
    gate = jax.nn.softmax(best_s, axis=-1)
    nb = t // PEER_BLOCK

    def expert_block(args):
        xb, ib, gb = args
        act = jnp.einsum('td,thkd->thk', xb.astype(jnp.float32), u_tab[ib].astype(jnp.float32))
        wgt = jax.nn.gelu(act) * gb
        return jnp.einsum('thk,thkd->td', wgt, v_tab[ib].astype(jnp.float32))

    y = lax.map(expert_block, (xt.reshape(nb, PEER_BLOCK, d),
                               idx.reshape(nb, PEER_BLOCK, PEER_HEADS, PEER_TOPK),
                               gate.reshape(nb, PEER_BLOCK, PEER_HEADS, PEER_TOPK)))
    return y.reshape(b, s_len, d)


def kernel(x, norm1_g, w_in, sb_qn_g, sb_kn_g, ml_conv_w, ml_conv_b, ml_i_b, ml_f_b, sg_vn_g, sg_w, sg_b,
           out_g, w_out, norm2_g, peer_wq, peer_keys, peer_u, peer_v):
    for l in range(DEPTH):
        x = x + _mixing_sublayer(x, norm1_g[l], w_in[l], sb_qn_g[l], sb_kn_g[l], ml_conv_w[l], ml_conv_b[l],
                                 ml_i_b[l], ml_f_b[l], sg_vn_g[l], sg_w[l], sg_b[l], out_g[l], w_out[l])
        xn = _rms_norm(x, norm2_g[l])
        x = x + _peer_ffn(xn, peer_wq[l], peer_keys[l], peer_u[l], peer_v[l])
    return x
```

```python
import functools

import jax
import jax.numpy as jnp
import numpy as np
from jax import lax
from jax.experimental import pallas as pl
from jax.experimental.pallas import tpu as pltpu

D_MODEL = 1024
DEPTH = 4
HEAD_DIM = 64
SB_HEADS = 6
ML_HEADS = 6
SG_GROUPS = 4
SB_WIDTH = SB_HEADS * HEAD_DIM
ML_WIDTH = ML_HEADS * HEAD_DIM
SG_WIDTH = SG_GROUPS * HEAD_DIM
MIX_HEADS = SB_HEADS + ML_HEADS + SG_GROUPS
MIX_WIDTH = SB_WIDTH + ML_WIDTH + SG_WIDTH
IN_SIZES = (SB_WIDTH, SB_WIDTH, SB_WIDTH, 2 * ML_WIDTH, ML_WIDTH, ML_WIDTH, ML_HEADS, ML_HEADS, 2 * SG_WIDTH)
IN_WIDTH = sum(IN_SIZES)
Q_BLOCK = 128
ML_CHUNK = 128
SG_CHUNK = 128
CONV_WIDTH = 4
PEER_HEADS = 8
PEER_NKEYS = 128
PEER_EXPERTS = PEER_NKEYS * PEER_NKEYS
PEER_SUBDIM = 128
PEER_TOPK = 16
PEER_BLOCK = 128
NORM_EPS = 1e-6

VMEM_LIMIT_BYTES = 56 * 1024 * 1024


def _norm_matmul_body(x_ref, g_ref, w_ref, o_ref):
    x = x_ref[...]
    xn = x * lax.rsqrt(jnp.mean(x * x, axis=-1, keepdims=True) + NORM_EPS) * g_ref[...]
    o_ref[...] = jnp.dot(xn.astype(jnp.bfloat16), w_ref[...], preferred_element_type=jnp.float32)


def norm_matmul(x, g, w, tm=512):
    t, d = x.shape
    n = w.shape[1]
    return pl.pallas_call(
        _norm_matmul_body,
        grid=(t // tm,),
        in_specs=[
            pl.BlockSpec((tm, d), lambda i: (i, 0)),
            pl.BlockSpec((1, d), lambda i: (0, 0)),
            pl.BlockSpec((d, n), lambda i: (0, 0)),
        ],
        out_specs=pl.BlockSpec((tm, n), lambda i: (i, 0)),
        out_shape=jax.ShapeDtypeStruct((t, n), jnp.float32),
        compiler_params=pltpu.CompilerParams(
            dimension_semantics=("parallel",), vmem_limit_bytes=VMEM_LIMIT_BYTES),
        name="norm_matmul",
    )(x, g.reshape(1, d), w)


LANES = 128
PAIRS = PEER_HEADS * PEER_TOPK
MXU_K = 256


def _peer_act_body(xn_ref, u_ref, idx_ref, gate_ref, o_ref, sel_ref, *, groups):
    ei = pl.program_id(1)

    @pl.when(ei == 0)
    def _():
        sel_ref[...] = jnp.zeros_like(sel_ref)

    idx = idx_ref[...]
    hi = idx >> 7
    lo = idx & (LANES - 1)
    sel = sel_ref[...]
    xn = xn_ref[...]
    for gg in range(groups * LANES // MXU_K):
        act = lax.dot_general(xn, u_ref[gg * MXU_K:(gg + 1) * MXU_K, :], (((1,), (1,)), ((), ())),
                              preferred_element_type=jnp.float32)
        for h in range(MXU_K // LANES):
            g = gg * (MXU_K // LANES) + h
            picked = jnp.take_along_axis(act[:, h * LANES:(h + 1) * LANES], lo, axis=1)
            sel = jnp.where(hi == ei * groups + g, picked, sel)
    sel_ref[...] = sel

    @pl.when(ei == pl.num_programs(1) - 1)
    def _():
        o_ref[...] = jax.nn.gelu(sel_ref[...]) * gate_ref[...]


def peer_act(xn_bf, u_bf, idx, gate, tm=512, te=1024):
    t, d = xn_bf.shape
    e = u_bf.shape[0]
    return pl.pallas_call(
        functools.partial(_peer_act_body, groups=te // LANES),
        grid=(t // tm, e // te),
        in_specs=[
            pl.BlockSpec((tm, d), lambda i, j: (i, 0)),
            pl.BlockSpec((te, d), lambda i, j: (j, 0)),
            pl.BlockSpec((tm, PAIRS), lambda i, j: (i, 0)),
            pl.BlockSpec((tm, PAIRS), lambda i, j: (i, 0)),
        ],
        out_specs=pl.BlockSpec((tm, PAIRS), lambda i, j: (i, 0)),
        out_shape=jax.ShapeDtypeStruct((t, PAIRS), jnp.float32),
        scratch_shapes=[pltpu.VMEM((tm, PAIRS), jnp.float32)],
        compiler_params=pltpu.CompilerParams(
            dimension_semantics=("parallel", "arbitrary"), vmem_limit_bytes=VMEM_LIMIT_BYTES),
        name="peer_act",
    )(xn_bf, u_bf, idx, gate)


SUBLANES = 8
SCATTER_TOKENS = 64
STAGE_PITCH = LANES + SUBLANES


def _peer_out_body(wgt_ref, idx_ref, v_ref, x_ref, o_ref, w_ref, stage_ref, *, kgroups):
    ei = pl.program_id(1)
    tm = wgt_ref.shape[0]

    @pl.when(ei == 0)
    def _():
        sub_iota = lax.broadcasted_iota(jnp.int32, (LANES, PAIRS), 0)

        def block(sb, carry):
            t0 = pl.multiple_of(sb * SCATTER_TOKENS, SCATTER_TOKENS)

            def tokens(tb, c2):
                r0 = pl.multiple_of(t0 + tb * SUBLANES, SUBLANES)
                idx8 = idx_ref[pl.ds(r0, SUBLANES), :]
                w8 = wgt_ref[pl.ds(r0, SUBLANES), :]
                for s in range(SUBLANES):
                    idx_row = idx8[s:s + 1, :]
                    a = jnp.where(sub_iota == (idx_row >> 7), 1.0, 0.0).astype(jnp.bfloat16)
                    b = jnp.where(sub_iota == (idx_row & (LANES - 1)), w8[s:s + 1, :], 0.0).astype(jnp.bfloat16)
                    row0 = pl.multiple_of((tb * SUBLANES + s) * STAGE_PITCH, SUBLANES)
                    stage_ref[pl.ds(row0, LANES), :] = lax.dot_general(
                        a, b, (((1,), (1,)), ((), ())), preferred_element_type=jnp.float32)
                return c2

            lax.fori_loop(0, SCATTER_TOKENS // SUBLANES, tokens, 0)

            def regroup(ip, c2):
                for half in range(MXU_K // LANES):
                    rows = stage_ref[pl.ds(ip * (MXU_K // LANES) + half, SCATTER_TOKENS, stride=STAGE_PITCH), :]
                    w_ref[ip, pl.ds(t0, SCATTER_TOKENS), half * LANES:(half + 1) * LANES] = rows.astype(jnp.bfloat16)
                return c2

            lax.fori_loop(0, LANES * LANES // MXU_K, regroup, 0)
            return carry

        lax.fori_loop(0, tm // SCATTER_TOKENS, block, 0)
        o_ref[...] = x_ref[...]

    acc = o_ref[...]
    for g in range(kgroups):
        acc = acc + jnp.dot(w_ref[ei * kgroups + g], v_ref[g * MXU_K:(g + 1) * MXU_K, :],
                            preferred_element_type=jnp.float32)
    o_ref[...] = acc


def peer_out(wgt, idx, v_bf, x, tm=512, te=1024):
    t, d = x.shape
    e = v_bf.shape[0]
    return pl.pallas_call(
        functools.partial(_peer_out_body, kgroups=te // MXU_K),
        grid=(t // tm, e // te),
        in_specs=[
            pl.BlockSpec((tm, PAIRS), lambda i, j: (i, 0)),
            pl.BlockSpec((tm, PAIRS), lambda i, j: (i, 0)),
            pl.BlockSpec((te, d), lambda i, j: (j, 0)),
            pl.BlockSpec((tm, d), lambda i, j: (i, 0)),
        ],
        out_specs=pl.BlockSpec((tm, d), lambda i, j: (i, 0)),
        out_shape=jax.ShapeDtypeStruct((t, d), jnp.float32),
        scratch_shapes=[pltpu.VMEM((e // MXU_K, tm, MXU_K), jnp.bfloat16),
                        pltpu.VMEM((SCATTER_TOKENS * STAGE_PITCH, LANES), jnp.float32)],
        compiler_params=pltpu.CompilerParams(
            dimension_semantics=("parallel", "arbitrary"), vmem_limit_bytes=VMEM_LIMIT_BYTES),
        name="peer_out",
    )(wgt, idx, v_bf, x)


def _rms_norm(x, g):
    x32 = x.astype(jnp.float32)
    return x32 * lax.rsqrt(jnp.mean(x32 * x32, axis=-1, keepdims=True) + NORM_EPS) * g.astype(jnp.float32)


def _split_heads(t, n_heads):
    b, s, _ = t.shape
    return t.reshape(b, s, n_heads, HEAD_DIM).transpose(0, 2, 1, 3)


def _causal_conv(t, w, b):
    s = t.shape[1]
    tp = jnp.pad(t, ((0, 0), (CONV_WIDTH - 1, 0), (0, 0)))
    out = b.astype(jnp.float32)
    for j in range(CONV_WIDTH):
        out = out + tp[:, j:j + s] * w[j]
    return out


def _stick_breaking_attention(q, k, v):
    s_len = q.shape[2]
    scale = HEAD_DIM ** -0.5
    outs = []
    for blk in range(s_len // Q_BLOCK):
        start = blk * Q_BLOCK
        end = start + Q_BLOCK
        z = jnp.einsum('bhqd,bhkd->bhqk', q[:, :, start:end], k[:, :, :end]) * scale
        before = jnp.arange(end)[None, :] < (start + jnp.arange(Q_BLOCK))[:, None]
        log_keep = jnp.where(before, -jax.nn.softplus(z), 0.0)
        log_keep_after = lax.cumsum(log_keep, axis=3, reverse=True) - log_keep
        w = jnp.where(before, jnp.exp(jax.nn.log_sigmoid(z) + log_keep_after), 0.0)
        outs.append(jnp.einsum('bhqk,bhkd->bhqd', w, v[:, :, :end]))
    return jnp.concatenate(outs, axis=2)


def _mlstm_chunkwise(q, k, v, i_pre, f_pre):
    b, h, s_len, d = q.shape
    nc = s_len // ML_CHUNK
    log_f = jax.nn.log_sigmoid(f_pre)

    def chunks(t):
        return jnp.moveaxis(t.reshape(b, h, nc, ML_CHUNK, *t.shape[3:]), 2, 0)

    tril = jnp.tril(jnp.ones((ML_CHUNK, ML_CHUNK), dtype=bool))

    def step(carry, inp):
        c_mat, n_vec, m = carry
        qc, kc, vc, ic, lfc = inp
        bcum = jnp.cumsum(lfc, axis=-1)
        dmat = jnp.where(tril, bcum[..., :, None] - bcum[..., None, :] + ic[..., None, :], -jnp.inf)
        m_inter = bcum + m[..., None]
        m_t = jnp.maximum(m_inter, jnp.max(dmat, axis=-1))
        inter_scale = jnp.exp(m_inter - m_t)
        qk = jnp.einsum('bhtd,bhsd->bhts', qc, kc) * jnp.exp(dmat - m_t[..., None])
        num = jnp.einsum('bhts,bhsd->bhtd', qk, vc) + inter_scale[..., None] * jnp.einsum('bhtd,bhde->bhte', qc, c_mat)
        den = jnp.sum(qk, axis=-1) + inter_scale * jnp.einsum('bhtd,bhd->bht', qc, n_vec)
        h_out = num / jnp.maximum(jnp.abs(den), jnp.exp(-m_t))[..., None]
        b_last = bcum[..., -1]
        g_s = b_last[..., None] - bcum + ic
        m_new = jnp.maximum(b_last + m, jnp.max(g_s, axis=-1))
        w_s = jnp.exp(g_s - m_new[..., None])
        decay = jnp.exp(b_last + m - m_new)
        c_new = decay[..., None, None] * c_mat + jnp.einsum('bhs,bhsd,bhse->bhde', w_s, kc, vc)
        n_new = decay[..., None] * n_vec + jnp.einsum('bhs,bhsd->bhd', w_s, kc)
        return (c_new, n_new, m_new), h_out

    init = (jnp.zeros((b, h, d, d), jnp.float32), jnp.zeros((b, h, d), jnp.float32), jnp.zeros((b, h), jnp.float32))
    _, hs = lax.scan(step, init, (chunks(q), chunks(k), chunks(v), chunks(i_pre), chunks(log_f)))
    return jnp.moveaxis(hs, 0, 2).reshape(b, h, s_len, d)


def _spatial_gating(u, v, w_s, b_s):
    b, s_len, _ = u.shape
    nc = s_len // SG_CHUNK
    tril = jnp.tril(jnp.ones((SG_CHUNK, SG_CHUNK), dtype=jnp.float32))
    v5 = v.reshape(b, nc, SG_CHUNK, SG_GROUPS, HEAD_DIM)
    gate = jnp.einsum('gts,bcsgd->bctgd', w_s.astype(jnp.float32) * tril, v5) + b_s.astype(jnp.float32).T[:, :, None]
    return (u.reshape(b, nc, SG_CHUNK, SG_GROUPS, HEAD_DIM) * gate).reshape(b, s_len, SG_GROUPS, HEAD_DIM)


def _mixing_sublayer(x, norm_g, w_in, sb_qn_g, sb_kn_g, ml_conv_w, ml_conv_b, ml_i_b, ml_f_b,
                     sg_vn_g, sg_w, sg_b, out_g, w_out):
    b, s_len, d = x.shape
    n_pad = -IN_WIDTH % 128
    w_pad = jnp.pad(w_in, ((0, 0), (0, n_pad))).astype(jnp.bfloat16)
    z = norm_matmul(x.reshape(b * s_len, d), norm_g, w_pad)[:, :IN_WIDTH].reshape(b, s_len, IN_WIDTH)
    cuts = [int(c) for c in np.cumsum(IN_SIZES)[:-1]]
    sb_q, sb_k, sb_v, ml_qk, ml_v, ml_o, ml_i, ml_f, sg_uv = jnp.split(z, cuts, axis=-1)
    head_g = out_g.astype(jnp.float32).reshape(MIX_HEADS, HEAD_DIM)

    q = _rms_norm(_split_heads(sb_q, SB_HEADS), sb_qn_g)
    k = _rms_norm(_split_heads(sb_k, SB_HEADS), sb_kn_g)
    y_sb = _stick_breaking_attention(q, k, _split_heads(sb_v, SB_HEADS)).transpose(0, 2, 1, 3)
    y_sb = _rms_norm(y_sb, head_g[:SB_HEADS])

    qk = jax.nn.silu(_causal_conv(ml_qk, ml_conv_w.astype(jnp.float32), ml_conv_b))
    mq = _split_heads(qk[..., :ML_WIDTH], ML_HEADS)
    mk = _split_heads(qk[..., ML_WIDTH:], ML_HEADS) * (HEAD_DIM ** -0.5)
    i_pre = (ml_i + ml_i_b.astype(jnp.float32)).transpose(0, 2, 1)
    f_pre = (ml_f + ml_f_b.astype(jnp.float32)).transpose(0, 2, 1)
    y_ml = _mlstm_chunkwise(mq, mk, _split_heads(ml_v, ML_HEADS), i_pre, f_pre).transpose(0, 2, 1, 3)
    o_gate = jax.nn.sigmoid(ml_o).reshape(b, s_len, ML_HEADS, HEAD_DIM)
    y_ml = _rms_norm(y_ml, head_g[SB_HEADS:SB_HEADS + ML_HEADS]) * o_gate

    uv = jax.nn.gelu(sg_uv)
    u = uv[..., :SG_WIDTH]
    v = _rms_norm(uv[..., SG_WIDTH:].reshape(b, s_len, SG_GROUPS, HEAD_DIM),
                  sg_vn_g.reshape(SG_GROUPS, HEAD_DIM)).reshape(b, s_len, SG_WIDTH)
    y_sg = _rms_norm(_spatial_gating(u, v, sg_w, sg_b), head_g[SB_HEADS + ML_HEADS:])

    y = jnp.concatenate([y_sb, y_ml, y_sg], axis=2).reshape(b, s_len, MIX_WIDTH)
    return jnp.matmul(y, w_out)


def _peer_ffn(x, xn, wq, sub_keys, u_tab, v_tab):
    b, s_len, d = xn.shape
    t = b * s_len
    xt = xn.reshape(t, d)
    q = jnp.matmul(xt, wq).astype(jnp.float32).reshape(t, PEER_HEADS, 2, PEER_SUBDIM)
    scores = jnp.einsum('thcd,hcnd->thcn', q, sub_keys.astype(jnp.float32))
    sv, si = lax.top_k(scores, PEER_TOPK)
    cand_s = (sv[:, :, 0, :, None] + sv[:, :, 1, None, :]).reshape(t, PEER_HEADS, PEER_TOPK * PEER_TOPK)
    cand_i = (si[:, :, 0, :, None] * PEER_NKEYS + si[:, :, 1, None, :]).reshape(t, PEER_HEADS, PEER_TOPK * PEER_TOPK)
    best_s, best_p = lax.top_k(cand_s, PEER_TOPK)
    idx = jnp.take_along_axis(cand_i, best_p, axis=-1)
    gate = jax.nn.softmax(best_s, axis=-1)
    idx2 = idx.reshape(t, PAIRS).astype(jnp.int32)
    wgt = peer_act(xt.astype(jnp.bfloat16), u_tab.astype(jnp.bfloat16), idx2, gate.reshape(t, PAIRS))
    return peer_out(wgt, idx2, v_tab.astype(jnp.bfloat16), x.reshape(t, d)).reshape(b, s_len, d)


def kernel(x, norm1_g, w_in, sb_qn_g, sb_kn_g, ml_conv_w, ml_conv_b, ml_i_b, ml_f_b, sg_vn_g, sg_w, sg_b,
           out_g, w_out, norm2_g, peer_wq, peer_keys, peer_u, peer_v):
    for l in range(DEPTH):
        x = x + _mixing_sublayer(x, norm1_g[l], w_in[l], sb_qn_g[l], sb_kn_g[l], ml_conv_w[l], ml_conv_b[l],
                                 ml_i_b[l], ml_f_b[l], sg_vn_g[l], sg_w[l], sg_b[l], out_g[l], w_out[l])
        xn = _rms_norm(x, norm2_g[l])
        x = _peer_ffn(x, xn, peer_wq[l], peer_keys[l], peer_u[l], peer_v[l])
    return x
```

```python
import functools

import jax
import jax.numpy as jnp
import numpy as np
from jax import lax
from jax.experimental import pallas as pl
from jax.experimental.pallas import tpu as pltpu

D_MODEL = 1024
DEPTH = 4
HEAD_DIM = 64
SB_HEADS = 6
ML_HEADS = 6
SG_GROUPS = 4
SB_WIDTH = SB_HEADS * HEAD_DIM
ML_WIDTH = ML_HEADS * HEAD_DIM
SG_WIDTH = SG_GROUPS * HEAD_DIM
MIX_HEADS = SB_HEADS + ML_HEADS + SG_GROUPS
MIX_WIDTH = SB_WIDTH + ML_WIDTH + SG_WIDTH
IN_SIZES = (SB_WIDTH, SB_WIDTH, SB_WIDTH, 2 * ML_WIDTH, ML_WIDTH, ML_WIDTH, ML_HEADS, ML_HEADS, 2 * SG_WIDTH)
IN_WIDTH = sum(IN_SIZES)
Q_BLOCK = 128
ML_CHUNK = 128
SG_CHUNK = 128
CONV_WIDTH = 4
PEER_HEADS = 8
PEER_NKEYS = 128
PEER_EXPERTS = PEER_NKEYS * PEER_NKEYS
PEER_SUBDIM = 128
PEER_TOPK = 16
PEER_BLOCK = 128
NORM_EPS = 1e-6

VMEM_LIMIT_BYTES = 56 * 1024 * 1024


LANES = 128
PAIRS = PEER_HEADS * PEER_TOPK
MXU_K = 256


def _peer_act_body(xn_ref, u_ref, idx_ref, gate_ref, o_ref, sel_ref, *, groups):
    ei = pl.program_id(1)

    @pl.when(ei == 0)
    def _():
        sel_ref[...] = jnp.zeros_like(sel_ref)

    idx = idx_ref[...]
    hi = idx >> 7
    lo = idx & (LANES - 1)
    sel = sel_ref[...]
    xn = xn_ref[...]
    for gg in range(groups * LANES // MXU_K):
        act = lax.dot_general(xn, u_ref[gg * MXU_K:(gg + 1) * MXU_K, :], (((1,), (1,)), ((), ())),
                              preferred_element_type=jnp.float32)
        for h in range(MXU_K // LANES):
            g = gg * (MXU_K // LANES) + h
            picked = jnp.take_along_axis(act[:, h * LANES:(h + 1) * LANES], lo, axis=1)
            sel = jnp.where(hi == ei * groups + g, picked, sel)
    sel_ref[...] = sel

    @pl.when(ei == pl.num_programs(1) - 1)
    def _():
        o_ref[...] = jax.nn.gelu(sel_ref[...]) * gate_ref[...]


def peer_act(xn_bf, u_bf, idx, gate, tm=512, te=1024):
    t, d = xn_bf.shape
    e = u_bf.shape[0]
    return pl.pallas_call(
        functools.partial(_peer_act_body, groups=te // LANES),
        grid=(t // tm, e // te),
        in_specs=[
            pl.BlockSpec((tm, d), lambda i, j: (i, 0)),
            pl.BlockSpec((te, d), lambda i, j: (j, 0)),
            pl.BlockSpec((tm, PAIRS), lambda i, j: (i, 0)),
            pl.BlockSpec((tm, PAIRS), lambda i, j: (i, 0)),
        ],
        out_specs=pl.BlockSpec((tm, PAIRS), lambda i, j: (i, 0)),
        out_shape=jax.ShapeDtypeStruct((t, PAIRS), jnp.float32),
        scratch_shapes=[pltpu.VMEM((tm, PAIRS), jnp.float32)],
        compiler_params=pltpu.CompilerParams(
            dimension_semantics=("parallel", "arbitrary"), vmem_limit_bytes=VMEM_LIMIT_BYTES),
        name="peer_act",
    )(xn_bf, u_bf, idx, gate)


SUBLANES = 8
SCATTER_TOKENS = 64
STAGE_PITCH = LANES + SUBLANES


def _peer_out_body(wgt_ref, idx_ref, v_ref, x_ref, o_ref, w_ref, stage_ref, *, kgroups):
    ei = pl.program_id(1)
    tm = wgt_ref.shape[0]

    @pl.when(ei == 0)
    def _():
        sub_iota = lax.broadcasted_iota(jnp.int32, (LANES, PAIRS), 0)

        def block(sb, carry):
            t0 = pl.multiple_of(sb * SCATTER_TOKENS, SCATTER_TOKENS)

            def tokens(tb, c2):
                r0 = pl.multiple_of(t0 + tb * SUBLANES, SUBLANES)
                idx8 = idx_ref[pl.ds(r0, SUBLANES), :]
                w8 = wgt_ref[pl.ds(r0, SUBLANES), :]
                for s in range(SUBLANES):
                    idx_row = idx8[s:s + 1, :]
                    a = jnp.where(sub_iota == (idx_row >> 7), 1.0, 0.0).astype(jnp.bfloat16)
                    b = jnp.where(sub_iota == (idx_row & (LANES - 1)), w8[s:s + 1, :], 0.0).astype(jnp.bfloat16)
                    row0 = pl.multiple_of((tb * SUBLANES + s) * STAGE_PITCH, SUBLANES)
                    stage_ref[pl.ds(row0, LANES), :] = lax.dot_general(
                        a, b, (((1,), (1,)), ((), ())), preferred_element_type=jnp.float32)
                return c2

            lax.fori_loop(0, SCATTER_TOKENS // SUBLANES, tokens, 0)

            def regroup(ip, c2):
                for half in range(MXU_K // LANES):
                    rows = stage_ref[pl.ds(ip * (MXU_K // LANES) + half, SCATTER_TOKENS, stride=STAGE_PITCH), :]
                    w_ref[ip, pl.ds(t0, SCATTER_TOKENS), half * LANES:(half + 1) * LANES] = rows.astype(jnp.bfloat16)
                return c2

            lax.fori_loop(0, LANES * LANES // MXU_K, regroup, 0)
            return carry

        lax.fori_loop(0, tm // SCATTER_TOKENS, block, 0)
        o_ref[...] = x_ref[...]

    acc = o_ref[...]
    for g in range(kgroups):
        acc = acc + jnp.dot(w_ref[ei * kgroups + g], v_ref[g * MXU_K:(g + 1) * MXU_K, :],
                            preferred_element_type=jnp.float32)
    o_ref[...] = acc


def peer_out(wgt, idx, v_bf, x, tm=512, te=1024):
    t, d = x.shape
    e = v_bf.shape[0]
    return pl.pallas_call(
        functools.partial(_peer_out_body, kgroups=te // MXU_K),
        grid=(t // tm, e // te),
        in_specs=[
            pl.BlockSpec((tm, PAIRS), lambda i, j: (i, 0)),
            pl.BlockSpec((tm, PAIRS), lambda i, j: (i, 0)),
            pl.BlockSpec((te, d), lambda i, j: (j, 0)),
            pl.BlockSpec((tm, d), lambda i, j: (i, 0)),
        ],
        out_specs=pl.BlockSpec((tm, d), lambda i, j: (i, 0)),
        out_shape=jax.ShapeDtypeStruct((t, d), jnp.float32),
        scratch_shapes=[pltpu.VMEM((e // MXU_K, tm, MXU_K), jnp.bfloat16),
                        pltpu.VMEM((SCATTER_TOKENS * STAGE_PITCH, LANES), jnp.float32)],
        compiler_params=pltpu.CompilerParams(
            dimension_semantics=("parallel", "arbitrary"), vmem_limit_bytes=VMEM_LIMIT_BYTES),
        name="peer_out",
    )(wgt, idx, v_bf, x)


Z_BLOCKS = 25
Z_WIDTH = Z_BLOCKS * LANES
GATE_ROWS = 16


def _first_half(shape):
    return lax.broadcasted_iota(jnp.int32, shape, len(shape) - 1) < HEAD_DIM


def _pair_rms(x, g):
    first = _first_half(x.shape)
    x2 = x * x
    s_a = jnp.sum(jnp.where(first, x2, 0.0), axis=-1, keepdims=True)
    s_b = jnp.sum(jnp.where(first, 0.0, x2), axis=-1, keepdims=True)
    return x * lax.rsqrt(jnp.where(first, s_a, s_b) * (1.0 / HEAD_DIM) + NORM_EPS) * g


def _softplus(z):
    return jnp.maximum(z, 0.0) + jnp.log(1.0 + jnp.exp(-jnp.abs(z)))


def _in_proj_body(x_ref, g_ref, w_ref, wg_ref, z_ref, gt_ref):
    x = x_ref[...]
    xn = (x * lax.rsqrt(jnp.mean(x * x, axis=-1, keepdims=True) + NORM_EPS) * g_ref[...]).astype(jnp.bfloat16)
    z_ref[...] = jnp.dot(xn, w_ref[...], preferred_element_type=jnp.float32)
    gt_ref[...] = lax.dot_general(wg_ref[...], xn, (((1,), (1,)), ((), ())), preferred_element_type=jnp.float32)


def in_proj(x, g, w_main, w_gate_t, tm=512):
    t, d = x.shape
    return pl.pallas_call(
        _in_proj_body,
        grid=(t // tm,),
        in_specs=[
            pl.BlockSpec((tm, d), lambda i: (i, 0)),
            pl.BlockSpec((1, d), lambda i: (0, 0)),
            pl.BlockSpec((d, Z_WIDTH), lambda i: (0, 0)),
            pl.BlockSpec((GATE_ROWS, d), lambda i: (0, 0)),
        ],
        out_specs=[pl.BlockSpec((tm, Z_WIDTH), lambda i: (i, 0)),
                   pl.BlockSpec((GATE_ROWS, tm), lambda i: (0, i))],
        out_shape=[jax.ShapeDtypeStruct((t, Z_WIDTH), jnp.float32),
                   jax.ShapeDtypeStruct((GATE_ROWS, t), jnp.float32)],
        compiler_params=pltpu.CompilerParams(
            dimension_semantics=("parallel",), vmem_limit_bytes=VMEM_LIMIT_BYTES),
        name="in_proj",
    )(x, g.reshape(1, d), w_main, w_gate_t)


ATTN_BLOCK = 128
PREP_ROWS = 256
EXP_ZERO_LOG = -104.0


def _sb_attn_body(q_ref, k_ref, v_ref, qg_ref, kg_ref, og_ref, mcat_ref, o_ref, kn_ref, vb_ref):
    qi = pl.program_id(2)
    blk = ATTN_BLOCK
    s_len = k_ref.shape[1]

    @pl.when(qi == 0)
    def _():
        def prep(r, carry):
            r0 = pl.multiple_of(r * PREP_ROWS, PREP_ROWS)
            kn_ref[pl.ds(r0, PREP_ROWS), :] = _pair_rms(k_ref[0, pl.ds(r0, PREP_ROWS), :],
                                                        kg_ref[...]).astype(jnp.bfloat16)
            vb_ref[pl.ds(r0, PREP_ROWS), :] = v_ref[0, pl.ds(r0, PREP_ROWS), :].astype(jnp.bfloat16)
            return carry

        lax.fori_loop(0, s_len // PREP_ROWS, prep, 0)

    qn = _pair_rms(q_ref[0], qg_ref[...]) * (HEAD_DIM ** -0.5)
    first = _first_half(qn.shape)
    row = lax.broadcasted_iota(jnp.int32, (blk, blk), 0)
    col = lax.broadcasted_iota(jnp.int32, (blk, blk), 1)
    strict = col < row
    mcat = mcat_ref[...]
    nt = (((1,), (1,)), ((), ()))

    def key_block(qh, j, c, acc, diagonal):
        k0 = pl.multiple_of(j * blk, blk)
        z = lax.dot_general(qh, kn_ref[pl.ds(k0, blk), :], nt, preferred_element_type=jnp.float32)
        lk = -_softplus(z)
        if diagonal:
            lk = jnp.where(strict, lk, 0.0)
        hi = lk.astype(jnp.bfloat16)
        lo = (lk - hi.astype(jnp.float32)).astype(jnp.bfloat16)
        ct = (jnp.dot(hi, mcat, preferred_element_type=jnp.float32)
              + jnp.dot(lo, mcat, preferred_element_type=jnp.float32))
        w = jnp.exp(z + c + ct[:, :blk])
        if diagonal:
            w = jnp.where(strict, w, 0.0)
        acc = acc + jnp.dot(w.astype(jnp.bfloat16), vb_ref[pl.ds(k0, blk), :], preferred_element_type=jnp.float32)
        return c + ct[:, blk:], acc

    accs = []
    for head in range(2):
        qh = jnp.where(first if head == 0 else jnp.logical_not(first), qn, 0.0).astype(jnp.bfloat16)
        zeros = jnp.zeros((blk, LANES), jnp.float32)
        c, acc = key_block(qh, qi, zeros, zeros, True)

        def cond(carry):
            j, go, _, _ = carry
            return jnp.logical_and(j >= 0, go > 0)

        def body(carry, qh=qh):
            j, _, c, acc = carry
            c, acc = key_block(qh, j, c, acc, False)
            return j - 1, (jnp.max(c) > EXP_ZERO_LOG).astype(jnp.int32), c, acc

        _, _, _, acc = lax.while_loop(cond, body, (qi - 1, jnp.int32(1), c, acc))
        accs.append(acc)

    y = jnp.where(first, accs[0], accs[1])
    o_ref[0] = _pair_rms(y, og_ref[...]).astype(o_ref.dtype)


def _suffix_and_total_ones(n):
    r = np.arange(n)
    suffix = (r[:, None] >= r[None, :]).astype(np.float32)
    return jnp.asarray(np.concatenate([suffix, np.ones((n, n), np.float32)], axis=1), jnp.bfloat16)


def sb_attention(z3, qn_g, kn_g, out_g_row):
    b, s_len, _ = z3.shape
    pairs = SB_HEADS // 2
    blk = ATTN_BLOCK
    g2 = lambda g: jnp.concatenate([g, g]).reshape(1, LANES).astype(jnp.float32)
    return pl.pallas_call(
        _sb_attn_body,
        grid=(b, pairs, s_len // blk),
        in_specs=[
            pl.BlockSpec((1, blk, LANES), lambda bi, p, qi: (bi, qi, p)),
            pl.BlockSpec((1, s_len, LANES), lambda bi, p, qi: (bi, 0, pairs + p)),
            pl.BlockSpec((1, s_len, LANES), lambda bi, p, qi: (bi, 0, 2 * pairs + p)),
            pl.BlockSpec((1, LANES), lambda bi, p, qi: (0, 0)),
            pl.BlockSpec((1, LANES), lambda bi, p, qi: (0, 0)),
            pl.BlockSpec((1, LANES), lambda bi, p, qi: (0, p)),
            pl.BlockSpec((blk, 2 * blk), lambda bi, p, qi: (0, 0)),
        ],
        out_specs=pl.BlockSpec((1, blk, LANES), lambda bi, p, qi: (bi, qi, p)),
        out_shape=jax.ShapeDtypeStruct((b, s_len, SB_WIDTH), jnp.bfloat16),
        scratch_shapes=[pltpu.VMEM((s_len, LANES), jnp.bfloat16), pltpu.VMEM((s_len, LANES), jnp.bfloat16)],
        compiler_params=pltpu.CompilerParams(
            dimension_semantics=("parallel", "parallel", "arbitrary"), vmem_limit_bytes=VMEM_LIMIT_BYTES),
        name="sb_attention",
    )(z3, z3, z3, g2(qn_g), g2(kn_g), out_g_row, _suffix_and_total_ones(blk))


def _lane_cumsum(x):
    lane = lax.broadcasted_iota(jnp.int32, x.shape, 1)
    shift = 1
    while shift < LANES:
        x = x + jnp.where(lane >= shift, pltpu.roll(x, shift, 1), 0.0)
        shift *= 2
    return x


def _as_column(row):
    return jnp.broadcast_to(row, (LANES, LANES)).T


def _mlstm_body(zq_ref, zk_ref, zv_ref, zo_ref, gt_ref, gb_ref, cwq_ref, cwk_ref, cbq_ref, cbk_ref, og_ref,
                o_ref, c_ref, n_ref, m_ref):
    s_len = zq_ref.shape[1]
    blk = ML_CHUNK
    c_ref[...] = jnp.zeros_like(c_ref)
    n_ref[...] = jnp.zeros_like(n_ref)
    m_ref[...] = jnp.zeros_like(m_ref)
    first = _first_half((blk, LANES))
    row = lax.broadcasted_iota(jnp.int32, (blk, blk), 0)
    col = lax.broadcasted_iota(jnp.int32, (blk, blk), 1)
    causal = col <= row
    same_head = (row >> 6) == (col >> 6)
    first_row = _first_half((1, LANES))
    nt = (((1,), (1,)), ((), ()))

    def chunk(ci, carry):
        t0 = pl.multiple_of(ci * blk, blk)
        tp = pl.multiple_of(jnp.maximum(t0 - SUBLANES, 0), SUBLANES)

        def conv_silu(z_ref, w_ref, b_ref):
            prev = jnp.where(ci > 0, z_ref[0, pl.ds(tp, SUBLANES), :], 0.0)
            xx = jnp.concatenate([prev, z_ref[0, pl.ds(t0, blk), :]], axis=0)
            out = b_ref[...]
            for j in range(CONV_WIDTH):
                lag = SUBLANES - (CONV_WIDTH - 1) + j
                out = out + xx[lag:lag + blk, :] * w_ref[j:j + 1, :]
            return out * jax.nn.sigmoid(out)

        q = conv_silu(zq_ref, cwq_ref, cbq_ref)
        k = conv_silu(zk_ref, cwk_ref, cbk_ref) * (HEAD_DIM ** -0.5)
        qb = q.astype(jnp.bfloat16)
        kb = k.astype(jnp.bfloat16)
        vb = zv_ref[0, pl.ds(t0, blk), :].astype(jnp.bfloat16)

        g = gt_ref[ci, 0] + gb_ref[0]
        bcum = _lane_cumsum(-_softplus(-g))[2:4]
        ig = g[0:2]
        b_last = bcum[:, blk - 1:blk]
        gs = b_last - bcum + ig
        m_old = m_ref[0:2, 0:1]
        m_new = jnp.maximum(b_last + m_old, jnp.max(gs, axis=-1, keepdims=True))
        w_row = jnp.exp(gs - m_new)
        decay = jnp.exp(b_last + m_old - m_new)
        rowterm = ig - bcum

        nums, dens, mts, inters = [], [], [], []
        for h in range(2):
            qh = jnp.where(first if h == 0 else jnp.logical_not(first), q, 0.0).astype(jnp.bfloat16)
            qk = lax.dot_general(qh, kb, nt, preferred_element_type=jnp.float32)
            bcol = _as_column(bcum[h:h + 1])
            dmat = jnp.where(causal, bcol + rowterm[h:h + 1], -jnp.inf)
            m_int = bcol[:, 0:1] + m_old[h:h + 1]
            m_t = jnp.maximum(m_int, jnp.max(dmat, axis=-1, keepdims=True))
            pm = qk * jnp.exp(dmat - m_t)
            nums.append(jnp.dot(pm.astype(jnp.bfloat16), vb, preferred_element_type=jnp.float32))
            dens.append(jnp.sum(pm, axis=-1, keepdims=True))
            mts.append(m_t)
            inters.append(jnp.exp(m_int - m_t))

        pick = lambda ab: jnp.where(first, ab[0], ab[1])
        inter = pick(inters)
        qc = jnp.dot(qb, c_ref[...].astype(jnp.bfloat16), preferred_element_type=jnp.float32)
        qn = q * n_ref[0:1, :]
        qn = jnp.where(first, jnp.sum(jnp.where(first, qn, 0.0), axis=-1, keepdims=True),
                       jnp.sum(jnp.where(first, 0.0, qn), axis=-1, keepdims=True))
        den = pick(dens) + inter * qn
        hout = (pick(nums) + inter * qc) / jnp.maximum(jnp.abs(den), jnp.exp(-pick(mts)))
        y = _pair_rms(hout, og_ref[...]) * jax.nn.sigmoid(zo_ref[0, pl.ds(t0, blk), :])
        o_ref[0, pl.ds(t0, blk), :] = y.astype(o_ref.dtype)

        kw = k * jnp.where(first, _as_column(w_row[0:1]), _as_column(w_row[1:2]))
        upd = jnp.dot(kw.T.astype(jnp.bfloat16), vb, preferred_element_type=jnp.float32)
        dec = jnp.where(first_row, decay[0:1], decay[1:2])
        c_ref[...] = dec * c_ref[...] + jnp.where(same_head, upd, 0.0)
        n_ref[0:1, :] = dec * n_ref[0:1, :] + jnp.sum(kw, axis=0, keepdims=True)
        m_ref[0:2, :] = jnp.broadcast_to(m_new, (2, LANES))
        return carry

    lax.fori_loop(0, s_len // blk, chunk, 0)


def mlstm(z3, gates_t, ml_i_b, ml_f_b, conv_w, conv_b, out_g_row):
    b, s_len, _ = z3.shape
    pairs = ML_HEADS // 2
    blk = ML_CHUNK
    nc = s_len // blk
    gi = gates_t[:ML_HEADS].reshape(pairs, 2, b * nc, blk)
    gf = gates_t[ML_HEADS:2 * ML_HEADS].reshape(pairs, 2, b * nc, blk)
    gt = jnp.concatenate([gi, gf, jnp.zeros((pairs, SUBLANES - 4, b * nc, blk), jnp.float32)], axis=1)
    gt = gt.transpose(2, 0, 1, 3)
    gb = jnp.concatenate([ml_i_b.reshape(pairs, 2), ml_f_b.reshape(pairs, 2),
                          jnp.zeros((pairs, SUBLANES - 4), jnp.float32)], axis=1).reshape(pairs, SUBLANES, 1)
    zspec = lambda off: pl.BlockSpec((1, s_len, LANES), lambda bi, p: (bi, 0, off + p))
    return pl.pallas_call(
        _mlstm_body,
        grid=(b, pairs),
        in_specs=[
            zspec(9), zspec(12), zspec(15), zspec(18),
            pl.BlockSpec((nc, 1, SUBLANES, blk), lambda bi, p: (bi, p, 0, 0)),
            pl.BlockSpec((1, SUBLANES, 1), lambda bi, p: (p, 0, 0)),
            pl.BlockSpec((CONV_WIDTH, LANES), lambda bi, p: (0, p)),
            pl.BlockSpec((CONV_WIDTH, LANES), lambda bi, p: (0, pairs + p)),
            pl.BlockSpec((1, LANES), lambda bi, p: (0, p)),
            pl.BlockSpec((1, LANES), lambda bi, p: (0, pairs + p)),
            pl.BlockSpec((1, LANES), lambda bi, p: (0, SB_HEADS // 2 + p)),
        ],
        out_specs=pl.BlockSpec((1, s_len, LANES), lambda bi, p: (bi, 0, p)),
        out_shape=jax.ShapeDtypeStruct((b, s_len, ML_WIDTH), jnp.bfloat16),
        scratch_shapes=[pltpu.VMEM((LANES, LANES), jnp.float32), pltpu.VMEM((SUBLANES, LANES), jnp.float32),
                        pltpu.VMEM((SUBLANES, LANES), jnp.float32)],
        compiler_params=pltpu.CompilerParams(
            dimension_semantics=("parallel", "parallel"), vmem_limit_bytes=VMEM_LIMIT_BYTES),
        name="mlstm",
    )(z3, z3, z3, z3, gt, gb, conv_w, conv_w, conv_b.reshape(1, -1), conv_b.reshape(1, -1), out_g_row)


def _sg_body(u_ref, v_ref, w_ref, bias_ref, vg_ref, og_ref, o_ref):
    blk = SG_CHUNK
    first = _first_half((blk, LANES))
    row = lax.broadcasted_iota(jnp.int32, (blk, blk), 0)
    col = lax.broadcasted_iota(jnp.int32, (blk, blk), 1)
    u = jax.nn.gelu(u_ref[0])
    v = _pair_rms(jax.nn.gelu(v_ref[0]), vg_ref[...]).astype(jnp.bfloat16)
    gates = [jnp.dot(jnp.where(col <= row, w_ref[g], 0.0).astype(jnp.bfloat16), v,
                     preferred_element_type=jnp.float32) for g in range(2)]
    y = u * (jnp.where(first, gates[0], gates[1]) + bias_ref[0])
    o_ref[0] = _pair_rms(y, og_ref[...]).astype(o_ref.dtype)


def spatial_gating(z3, sg_vn_g, sg_w, sg_b, out_g_row):
    b, s_len, _ = z3.shape
    pairs = SG_GROUPS // 2
    blk = SG_CHUNK
    bias = jnp.repeat(sg_b.astype(jnp.float32).reshape(pairs, 2, blk).transpose(0, 2, 1), HEAD_DIM, axis=2)
    return pl.pallas_call(
        _sg_body,
        grid=(b, s_len // blk, pairs),
        in_specs=[
            pl.BlockSpec((1, blk, LANES), lambda bi, ci, p: (bi, ci, 21 + p)),
            pl.BlockSpec((1, blk, LANES), lambda bi, ci, p: (bi, ci, 23 + p)),
            pl.BlockSpec((2, blk, blk), lambda bi, ci, p: (p, 0, 0)),
            pl.BlockSpec((1, blk, LANES), lambda bi, ci, p: (p, 0, 0)),
            pl.BlockSpec((1, LANES), lambda bi, ci, p: (0, p)),
            pl.BlockSpec((1, LANES), lambda bi, ci, p: (0, (SB_HEADS + ML_HEADS) // 2 + p)),
        ],
        out_specs=pl.BlockSpec((1, blk, LANES), lambda bi, ci, p: (bi, ci, p)),
        out_shape=jax.ShapeDtypeStruct((b, s_len, SG_WIDTH), jnp.bfloat16),
        compiler_params=pltpu.CompilerParams(
            dimension_semantics=("parallel", "parallel", "parallel"), vmem_limit_bytes=VMEM_LIMIT_BYTES),
        name="spatial_gating",
    )(z3, z3, sg_w.astype(jnp.float32), bias, sg_vn_g.reshape(1, SG_WIDTH).astype(jnp.float32), out_g_row)


def _out_proj_body(x_ref, a_ref, b_ref, c_ref, w_ref, o_ref):
    acc = x_ref[...]
    acc = acc + jnp.dot(a_ref[...], w_ref[0:SB_WIDTH, :], preferred_element_type=jnp.float32)
    acc = acc + jnp.dot(b_ref[...], w_ref[SB_WIDTH:SB_WIDTH + ML_WIDTH, :], preferred_element_type=jnp.float32)
    acc = acc + jnp.dot(c_ref[...], w_ref[SB_WIDTH + ML_WIDTH:, :], preferred_element_type=jnp.float32)
    o_ref[...] = acc


def out_proj(x, y_sb, y_ml, y_sg, w_bf, tm=512):
    t, d = x.shape
    return pl.pallas_call(
        _out_proj_body,
        grid=(t // tm,),
        in_specs=[
            pl.BlockSpec((tm, d), lambda i: (i, 0)),
            pl.BlockSpec((tm, SB_WIDTH), lambda i: (i, 0)),
            pl.BlockSpec((tm, ML_WIDTH), lambda i: (i, 0)),
            pl.BlockSpec((tm, SG_WIDTH), lambda i: (i, 0)),
            pl.BlockSpec((MIX_WIDTH, d), lambda i: (0, 0)),
        ],
        out_specs=pl.BlockSpec((tm, d), lambda i: (i, 0)),
        out_shape=jax.ShapeDtypeStruct((t, d), jnp.float32),
        compiler_params=pltpu.CompilerParams(
            dimension_semantics=("parallel",), vmem_limit_bytes=VMEM_LIMIT_BYTES),
        name="out_proj",
    )(x, y_sb, y_ml, y_sg, w_bf)


def mixing_sublayer(x, norm_g, w_in, sb_qn_g, sb_kn_g, ml_conv_w, ml_conv_b, ml_i_b, ml_f_b,
                    sg_vn_g, sg_w, sg_b, out_g, w_out):
    b, s_len, d = x.shape
    t = b * s_len
    gate0 = sum(IN_SIZES[:6])
    gate1 = gate0 + 2 * ML_HEADS
    w_main = jnp.concatenate([w_in[:, :gate0], w_in[:, gate1:]], axis=1).astype(jnp.bfloat16)
    w_gate_t = jnp.pad(w_in[:, gate0:gate1].T, ((0, GATE_ROWS - 2 * ML_HEADS), (0, 0))).astype(jnp.bfloat16)
    z, gates_t = in_proj(x.reshape(t, d), norm_g, w_main, w_gate_t)
    z3 = z.reshape(b, s_len, Z_WIDTH)
    out_g_row = out_g.reshape(1, MIX_WIDTH).astype(jnp.float32)
    y_sb = sb_attention(z3, sb_qn_g, sb_kn_g, out_g_row)
    y_ml = mlstm(z3, gates_t, ml_i_b.astype(jnp.float32), ml_f_b.astype(jnp.float32),
                 ml_conv_w.astype(jnp.float32), ml_conv_b.astype(jnp.float32), out_g_row)
    y_sg = spatial_gating(z3, sg_vn_g, sg_w, sg_b, out_g_row)
    out = out_proj(x.reshape(t, d), y_sb.reshape(t, SB_WIDTH), y_ml.reshape(t, ML_WIDTH),
                   y_sg.reshape(t, SG_WIDTH), w_out.astype(jnp.bfloat16))
    return out.reshape(b, s_len, d)


NEG_INF = float("-inf")
CAND_AB = [(a, b) for a in range(PEER_TOPK) for b in range(PEER_TOPK) if (a + 1) * (b + 1) <= PEER_TOPK]
N_CAND = len(CAND_AB)
CAND_ROWS = -(-N_CAND // SUBLANES) * SUBLANES


def _top16_rows(s, payload=None):
    n, tm = s.shape
    pos = lax.broadcasted_iota(jnp.int32, (n, tm), 0)
    slot = lax.broadcasted_iota(jnp.int32, (PEER_TOPK, tm), 0)
    vals = jnp.zeros((PEER_TOPK, tm), jnp.float32)
    picks = jnp.zeros((PEER_TOPK, tm), jnp.float32 if payload is not None else jnp.int32)
    for r in range(PEER_TOPK):
        m = jnp.max(s, axis=0, keepdims=True)
        at = jnp.min(jnp.where(s == m, pos, n), axis=0, keepdims=True)
        hit = pos == at
        if payload is not None:
            at = jnp.sum(jnp.where(hit, payload, 0.0), axis=0, keepdims=True)
        vals = jnp.where(slot == r, m, vals)
        picks = jnp.where(slot == r, at, picks)
        s = jnp.where(hit, NEG_INF, s)
    return vals, picks


def _select_rows(sel_bf, x):
    x1 = x.astype(jnp.bfloat16)
    r1 = x - x1.astype(jnp.float32)
    x2 = r1.astype(jnp.bfloat16)
    x3 = (r1 - x2.astype(jnp.float32)).astype(jnp.bfloat16)
    dot = lambda v: jnp.dot(sel_bf, v, preferred_element_type=jnp.float32)
    return dot(x1) + dot(x2) + dot(x3)


def _peer_route_body(x_ref, g_ref, wq_ref, keys_ref, sa_ref, sb_ref, xn_ref, idx_ref, gate_ref,
                     sc_ref, sv_ref, si_ref, oi_ref, og_ref):
    tm = x_ref.shape[0]
    x = x_ref[...]
    xn = (x * lax.rsqrt(jnp.mean(x * x, axis=-1, keepdims=True) + NORM_EPS) * g_ref[...]).astype(jnp.bfloat16)
    xn_ref[...] = xn
    q = jnp.dot(xn, wq_ref[...], preferred_element_type=jnp.float32).astype(jnp.bfloat16)
    nt = (((1,), (1,)), ((), ()))
    for hc in range(2 * PEER_HEADS):
        sc_ref[hc] = lax.dot_general(keys_ref[hc], q[:, hc * PEER_SUBDIM:(hc + 1) * PEER_SUBDIM], nt,
                                     preferred_element_type=jnp.float32)

    def first_stage(hc, carry):
        vals, picks = _top16_rows(sc_ref[hc])
        sv_ref[hc] = vals
        si_ref[hc] = picks.astype(jnp.float32)
        return carry

    lax.fori_loop(0, 2 * PEER_HEADS, first_stage, 0)

    cand_row = lax.broadcasted_iota(jnp.int32, (CAND_ROWS, tm), 0)

    def second_stage(h, carry):
        sv0, sv1 = sv_ref[2 * h], sv_ref[2 * h + 1]
        si0, si1 = si_ref[2 * h], si_ref[2 * h + 1]
        sa, sb = sa_ref[...], sb_ref[...]
        cand = jnp.where(cand_row < N_CAND, _select_rows(sa, sv0) + _select_rows(sb, sv1), NEG_INF)
        expert = (jnp.dot(sa, si0.astype(jnp.bfloat16), preferred_element_type=jnp.float32) * PEER_NKEYS
                  + jnp.dot(sb, si1.astype(jnp.bfloat16), preferred_element_type=jnp.float32))
        best, ids = _top16_rows(cand, payload=expert)
        e = jnp.exp(best - best[0:1])
        r0 = pl.multiple_of(h * PEER_TOPK, PEER_TOPK)
        og_ref[pl.ds(r0, PEER_TOPK), :] = e / jnp.sum(e, axis=0, keepdims=True)
        oi_ref[pl.ds(r0, PEER_TOPK), :] = ids
        return carry

    lax.fori_loop(0, PEER_HEADS, second_stage, 0)
    idx_ref[...] = oi_ref[...].T.astype(jnp.int32)
    gate_ref[...] = og_ref[...].T


def peer_route(x, g, wq_bf, keys_bf, tm=256):
    t, d = x.shape
    sa = np.zeros((CAND_ROWS, PEER_TOPK), np.float32)
    sb = np.zeros((CAND_ROWS, PEER_TOPK), np.float32)
    for r, (a, b) in enumerate(CAND_AB):
        sa[r, a] = 1.0
        sb[r, b] = 1.0
    nq = wq_bf.shape[1]
    hc = 2 * PEER_HEADS
    return pl.pallas_call(
        _peer_route_body,
        grid=(t // tm,),
        in_specs=[
            pl.BlockSpec((tm, d), lambda i: (i, 0)),
            pl.BlockSpec((1, d), lambda i: (0, 0)),
            pl.BlockSpec((d, nq), lambda i: (0, 0)),
            pl.BlockSpec((hc, PEER_NKEYS, PEER_SUBDIM), lambda i: (0, 0, 0)),
            pl.BlockSpec((CAND_ROWS, PEER_TOPK), lambda i: (0, 0)),
            pl.BlockSpec((CAND_ROWS, PEER_TOPK), lambda i: (0, 0)),
        ],
        out_specs=[pl.BlockSpec((tm, d), lambda i: (i, 0)),
                   pl.BlockSpec((tm, PAIRS), lambda i: (i, 0)),
                   pl.BlockSpec((tm, PAIRS), lambda i: (i, 0))],
        out_shape=[jax.ShapeDtypeStruct((t, d), jnp.bfloat16),
                   jax.ShapeDtypeStruct((t, PAIRS), jnp.int32),
                   jax.ShapeDtypeStruct((t, PAIRS), jnp.float32)],
        scratch_shapes=[pltpu.VMEM((hc, PEER_NKEYS, tm), jnp.float32),
                        pltpu.VMEM((hc, PEER_TOPK, tm), jnp.float32),
                        pltpu.VMEM((hc, PEER_TOPK, tm), jnp.float32),
                        pltpu.VMEM((PAIRS, tm), jnp.float32),
                        pltpu.VMEM((PAIRS, tm), jnp.float32)],
        compiler_params=pltpu.CompilerParams(
            dimension_semantics=("parallel",), vmem_limit_bytes=VMEM_LIMIT_BYTES),
        name="peer_route",
    )(x, g.reshape(1, d), wq_bf, keys_bf.reshape(hc, PEER_NKEYS, PEER_SUBDIM),
      jnp.asarray(sa, jnp.bfloat16), jnp.asarray(sb, jnp.bfloat16))


def peer_ffn(x, norm_g, wq, sub_keys, u_tab, v_tab):
    b, s_len, d = x.shape
    t = b * s_len
    x2 = x.reshape(t, d)
    xn_bf, idx, gate = peer_route(x2, norm_g, wq.astype(jnp.bfloat16), sub_keys.astype(jnp.bfloat16))
    wgt = peer_act(xn_bf, u_tab.astype(jnp.bfloat16), idx, gate)
    return peer_out(wgt, idx, v_tab.astype(jnp.bfloat16), x2).reshape(b, s_len, d)


def kernel(x, norm1_g, w_in, sb_qn_g, sb_kn_g, ml_conv_w, ml_conv_b, ml_i_b, ml_f_b, sg_vn_g, sg_w, sg_b,
           out_g, w_out, norm2_g, peer_wq, peer_keys, peer_u, peer_v):
    for l in range(DEPTH):
        x = mixing_sublayer(x, norm1_g[l], w_in[l], sb_qn_g[l], sb_kn_g[l], ml_conv_w[l], ml_conv_b[l],
                            ml_i_b[l], ml_f_b[l], sg_vn_g[l], sg_w[l], sg_b[l], out_g[l], w_out[l])
        x = peer_ffn(x, norm2_g[l], peer_wq[l], peer_keys[l], peer_u[l], peer_v[l])
    return x
```

```python
import functools

import jax
import jax.numpy as jnp
import numpy as np
from jax import lax
from jax.experimental import pallas as pl
from jax.experimental.pallas import tpu as pltpu

D_MODEL = 1024
DEPTH = 4
HEAD_DIM = 64
SB_HEADS = 6
ML_HEADS = 6
SG_GROUPS = 4
SB_WIDTH = SB_HEADS * HEAD_DIM
ML_WIDTH = ML_HEADS * HEAD_DIM
SG_WIDTH = SG_GROUPS * HEAD_DIM
MIX_HEADS = SB_HEADS + ML_HEADS + SG_GROUPS
MIX_WIDTH = SB_WIDTH + ML_WIDTH + SG_WIDTH
IN_SIZES = (SB_WIDTH, SB_WIDTH, SB_WIDTH, 2 * ML_WIDTH, ML_WIDTH, ML_WIDTH, ML_HEADS, ML_HEADS, 2 * SG_WIDTH)
IN_WIDTH = sum(IN_SIZES)
Q_BLOCK = 128
ML_CHUNK = 128
SG_CHUNK = 128
CONV_WIDTH = 4
PEER_HEADS = 8
PEER_NKEYS = 128
PEER_EXPERTS = PEER_NKEYS * PEER_NKEYS
PEER_SUBDIM = 128
PEER_TOPK = 16
PEER_BLOCK = 128
NORM_EPS = 1e-6

VMEM_LIMIT_BYTES = 56 * 1024 * 1024


LANES = 128
PAIRS = PEER_HEADS * PEER_TOPK
MXU_K = 256


def _peer_act_body(xn_ref, u_ref, idx_ref, gate_ref, o_ref, sel_ref, *, groups):
    ei = pl.program_id(1)

    @pl.when(ei == 0)
    def _():
        sel_ref[...] = jnp.zeros_like(sel_ref)

    idx = idx_ref[...]
    hi = idx >> 7
    lo = idx & (LANES - 1)
    sel = sel_ref[...]
    xn = xn_ref[...]
    for gg in range(groups * LANES // MXU_K):
        act = lax.dot_general(xn, u_ref[gg * MXU_K:(gg + 1) * MXU_K, :], (((1,), (1,)), ((), ())),
                              preferred_element_type=jnp.float32)
        for h in range(MXU_K // LANES):
            g = gg * (MXU_K // LANES) + h
            picked = jnp.take_along_axis(act[:, h * LANES:(h + 1) * LANES], lo, axis=1)
            sel = jnp.where(hi == ei * groups + g, picked, sel)
    sel_ref[...] = sel

    @pl.when(ei == pl.num_programs(1) - 1)
    def _():
        o_ref[...] = jax.nn.gelu(sel_ref[...]) * gate_ref[...]


def peer_act(xn_bf, u_bf, idx, gate, tm=512, te=4096):
    t, d = xn_bf.shape
    e = u_bf.shape[0]
    return pl.pallas_call(
        functools.partial(_peer_act_body, groups=te // LANES),
        grid=(t // tm, e // te),
        in_specs=[
            pl.BlockSpec((tm, d), lambda i, j: (i, 0)),
            pl.BlockSpec((te, d), lambda i, j: (j, 0)),
            pl.BlockSpec((tm, PAIRS), lambda i, j: (i, 0)),
            pl.BlockSpec((tm, PAIRS), lambda i, j: (i, 0)),
        ],
        out_specs=pl.BlockSpec((tm, PAIRS), lambda i, j: (i, 0)),
        out_shape=jax.ShapeDtypeStruct((t, PAIRS), jnp.float32),
        scratch_shapes=[pltpu.VMEM((tm, PAIRS), jnp.float32)],
        compiler_params=pltpu.CompilerParams(
            dimension_semantics=("parallel", "arbitrary"), vmem_limit_bytes=VMEM_LIMIT_BYTES),
        name="peer_act",
    )(xn_bf, u_bf, idx, gate)


SUBLANES = 8
SCATTER_TOKENS = 64
STAGE_PITCH = LANES + SUBLANES
REGROUP_UNROLL = 4


def _peer_out_body(wgt_ref, idx_ref, v_ref, x_ref, o_ref, w_ref, stage_ref, *, kgroups):
    ei = pl.program_id(1)
    tm = wgt_ref.shape[0]

    @pl.when(ei == 0)
    def _():
        sub_iota = lax.broadcasted_iota(jnp.int32, (LANES, PAIRS), 0).astype(jnp.float32).astype(jnp.bfloat16)
        one = jnp.ones((), jnp.bfloat16)
        zero = jnp.zeros((), jnp.bfloat16)

        def block(sb, carry):
            t0 = pl.multiple_of(sb * SCATTER_TOKENS, SCATTER_TOKENS)

            as_bf = lambda v: v.astype(jnp.float32).astype(jnp.bfloat16)
            for tb in range(SCATTER_TOKENS // SUBLANES):
                r0 = pl.multiple_of(t0 + tb * SUBLANES, SUBLANES)
                idx8 = idx_ref[pl.ds(r0, SUBLANES), :]
                hi8 = as_bf(idx8 >> 7)
                lo8 = as_bf(idx8 & (LANES - 1))
                w8 = wgt_ref[pl.ds(r0, SUBLANES), :].astype(jnp.bfloat16)
                for s in range(SUBLANES):
                    a = jnp.where(sub_iota == hi8[s:s + 1, :], one, zero)
                    b = jnp.where(sub_iota == lo8[s:s + 1, :], w8[s:s + 1, :], zero)
                    stage_ref[pl.ds((tb * SUBLANES + s) * STAGE_PITCH, LANES), :] = lax.dot_general(
                        a, b, (((1,), (1,)), ((), ())), preferred_element_type=jnp.float32)

            def regroup(ig, c2):
                for u in range(REGROUP_UNROLL):
                    ip = ig * REGROUP_UNROLL + u
                    for half in range(MXU_K // LANES):
                        rows = stage_ref[pl.ds(ip * (MXU_K // LANES) + half, SCATTER_TOKENS, stride=STAGE_PITCH), :]
                        w_ref[ip, pl.ds(t0, SCATTER_TOKENS), half * LANES:(half + 1) * LANES] = (
                            rows.astype(jnp.bfloat16))
                return c2

            lax.fori_loop(0, LANES * LANES // MXU_K // REGROUP_UNROLL, regroup, 0)
            return carry

        lax.fori_loop(0, tm // SCATTER_TOKENS, block, 0)
        o_ref[...] = x_ref[...]

    acc = o_ref[...]
    for g in range(kgroups):
        acc = acc + jnp.dot(w_ref[ei * kgroups + g], v_ref[g * MXU_K:(g + 1) * MXU_K, :],
                            preferred_element_type=jnp.float32)
    o_ref[...] = acc


def peer_out(wgt, idx, v_bf, x, tm=512, te=4096):
    t, d = x.shape
    e = v_bf.shape[0]
    return pl.pallas_call(
        functools.partial(_peer_out_body, kgroups=te // MXU_K),
        grid=(t // tm, e // te),
        in_specs=[
            pl.BlockSpec((tm, PAIRS), lambda i, j: (i, 0)),
            pl.BlockSpec((tm, PAIRS), lambda i, j: (i, 0)),
            pl.BlockSpec((te, d), lambda i, j: (j, 0)),
            pl.BlockSpec((tm, d), lambda i, j: (i, 0)),
        ],
        out_specs=pl.BlockSpec((tm, d), lambda i, j: (i, 0)),
        out_shape=jax.ShapeDtypeStruct((t, d), jnp.float32),
        scratch_shapes=[pltpu.VMEM((e // MXU_K, tm, MXU_K), jnp.bfloat16),
                        pltpu.VMEM((SCATTER_TOKENS * STAGE_PITCH, LANES), jnp.float32)],
        compiler_params=pltpu.CompilerParams(
            dimension_semantics=("parallel", "arbitrary"), vmem_limit_bytes=VMEM_LIMIT_BYTES),
        name="peer_out",
    )(wgt, idx, v_bf, x)


Z_BLOCKS = 25
Z_WIDTH = Z_BLOCKS * LANES
GATE_ROWS = 16


def _first_half(shape):
    return lax.broadcasted_iota(jnp.int32, shape, len(shape) - 1) < HEAD_DIM


def _pair_rms(x, g):
    first = _first_half(x.shape)
    x2 = x * x
    s_a = jnp.sum(jnp.where(first, x2, 0.0), axis=-1, keepdims=True)
    s_b = jnp.sum(jnp.where(first, 0.0, x2), axis=-1, keepdims=True)
    return x * lax.rsqrt(jnp.where(first, s_a, s_b) * (1.0 / HEAD_DIM) + NORM_EPS) * g


def _softplus(z):
    return jnp.maximum(z, 0.0) + jnp.log(1.0 + jnp.exp(-jnp.abs(z)))


def _in_proj_body(x_ref, g_ref, w_ref, wg_ref, z_ref, gt_ref):
    x = x_ref[...]
    xn = (x * lax.rsqrt(jnp.mean(x * x, axis=-1, keepdims=True) + NORM_EPS) * g_ref[...]).astype(jnp.bfloat16)
    z_ref[...] = jnp.dot(xn, w_ref[...], preferred_element_type=jnp.float32)
    gt_ref[...] = lax.dot_general(wg_ref[...], xn, (((1,), (1,)), ((), ())), preferred_element_type=jnp.float32)


def in_proj(x, g, w_main, w_gate_t, tm=512):
    t, d = x.shape
    return pl.pallas_call(
        _in_proj_body,
        grid=(t // tm,),
        in_specs=[
            pl.BlockSpec((tm, d), lambda i: (i, 0)),
            pl.BlockSpec((1, d), lambda i: (0, 0)),
            pl.BlockSpec((d, Z_WIDTH), lambda i: (0, 0)),
            pl.BlockSpec((GATE_ROWS, d), lambda i: (0, 0)),
        ],
        out_specs=[pl.BlockSpec((tm, Z_WIDTH), lambda i: (i, 0)),
                   pl.BlockSpec((GATE_ROWS, tm), lambda i: (0, i))],
        out_shape=[jax.ShapeDtypeStruct((t, Z_WIDTH), jnp.float32),
                   jax.ShapeDtypeStruct((GATE_ROWS, t), jnp.float32)],
        compiler_params=pltpu.CompilerParams(
            dimension_semantics=("parallel",), vmem_limit_bytes=VMEM_LIMIT_BYTES),
        name="in_proj",
    )(x, g.reshape(1, d), w_main, w_gate_t)


ATTN_BLOCK = 128
PREP_ROWS = 256
EXP_ZERO_LOG = -104.0


def _sb_attn_body(q_ref, k_ref, v_ref, qg_ref, kg_ref, og_ref, mcat_ref, o_ref, kn_ref, vb_ref):
    qi = pl.program_id(2)
    blk = ATTN_BLOCK
    s_len = k_ref.shape[1]

    @pl.when(qi == 0)
    def _():
        def prep(r, carry):
            r0 = pl.multiple_of(r * PREP_ROWS, PREP_ROWS)
            kn_ref[pl.ds(r0, PREP_ROWS), :] = _pair_rms(k_ref[0, pl.ds(r0, PREP_ROWS), :],
                                                        kg_ref[...]).astype(jnp.bfloat16)
            vb_ref[pl.ds(r0, PREP_ROWS), :] = v_ref[0, pl.ds(r0, PREP_ROWS), :].astype(jnp.bfloat16)
            return carry

        lax.fori_loop(0, s_len // PREP_ROWS, prep, 0)

    qn = _pair_rms(q_ref[0], qg_ref[...]) * (HEAD_DIM ** -0.5)
    first = _first_half(qn.shape)
    qh = jnp.concatenate([jnp.where(first, qn, 0.0), jnp.where(first, 0.0, qn)], axis=0).astype(jnp.bfloat16)
    win = 2 * blk
    q_pos = qi * blk + (lax.broadcasted_iota(jnp.int32, (2 * blk, win), 0) & (blk - 1))
    col = lax.broadcasted_iota(jnp.int32, (2 * blk, win), 1)
    mcat = mcat_ref[...]
    nt = (((1,), (1,)), ((), ()))

    def window(j, limit, c, acc):
        k0 = pl.multiple_of(jnp.maximum(j - 1, 0) * blk, blk)
        key_pos = k0 + col
        valid = jnp.logical_and(key_pos < q_pos, key_pos < limit)
        z = lax.dot_general(qh, kn_ref[pl.ds(k0, win), :], nt, preferred_element_type=jnp.float32)
        lk = jnp.where(valid, -_softplus(z), 0.0)
        hi = lk.astype(jnp.bfloat16)
        lo = (lk - hi.astype(jnp.float32)).astype(jnp.bfloat16)
        ct = (jnp.dot(hi, mcat, preferred_element_type=jnp.float32)
              + jnp.dot(lo, mcat, preferred_element_type=jnp.float32))
        w = jnp.where(valid, jnp.exp(z + c + ct[:, :win]), 0.0)
        acc = acc + jnp.dot(w.astype(jnp.bfloat16), vb_ref[pl.ds(k0, win), :], preferred_element_type=jnp.float32)
        return c + ct[:, win:], acc

    c, acc = window(qi, s_len, jnp.zeros((2 * blk, win), jnp.float32), jnp.zeros((2 * blk, LANES), jnp.float32))

    def cond(carry):
        j, go, _, _ = carry
        return jnp.logical_and(j >= 0, go > 0)

    def body(carry):
        j, _, c, acc = carry
        c, acc = window(j, (j + 1) * blk, c, acc)
        return j - 2, (jnp.max(c) > EXP_ZERO_LOG).astype(jnp.int32), c, acc

    _, _, _, acc = lax.while_loop(cond, body, (qi - 2, jnp.int32(1), c, acc))
    y = jnp.where(first, acc[:blk], acc[blk:])
    o_ref[0] = _pair_rms(y, og_ref[...]).astype(o_ref.dtype)


def _suffix_and_total_ones(n):
    r = np.arange(n)
    suffix = (r[:, None] >= r[None, :]).astype(np.float32)
    return jnp.asarray(np.concatenate([suffix, np.ones((n, n), np.float32)], axis=1), jnp.bfloat16)


def sb_attention(z3, qn_g, kn_g, out_g_row):
    b, s_len, _ = z3.shape
    pairs = SB_HEADS // 2
    blk = ATTN_BLOCK
    g2 = lambda g: jnp.concatenate([g, g]).reshape(1, LANES).astype(jnp.float32)
    return pl.pallas_call(
        _sb_attn_body,
        grid=(b, pairs, s_len // blk),
        in_specs=[
            pl.BlockSpec((1, blk, LANES), lambda bi, p, qi: (bi, qi, p)),
            pl.BlockSpec((1, s_len, LANES), lambda bi, p, qi: (bi, 0, pairs + p)),
            pl.BlockSpec((1, s_len, LANES), lambda bi, p, qi: (bi, 0, 2 * pairs + p)),
            pl.BlockSpec((1, LANES), lambda bi, p, qi: (0, 0)),
            pl.BlockSpec((1, LANES), lambda bi, p, qi: (0, 0)),
            pl.BlockSpec((1, LANES), lambda bi, p, qi: (0, p)),
            pl.BlockSpec((2 * blk, 4 * blk), lambda bi, p, qi: (0, 0)),
        ],
        out_specs=pl.BlockSpec((1, blk, LANES), lambda bi, p, qi: (bi, qi, p)),
        out_shape=jax.ShapeDtypeStruct((b, s_len, SB_WIDTH), jnp.bfloat16),
        scratch_shapes=[pltpu.VMEM((s_len, LANES), jnp.bfloat16), pltpu.VMEM((s_len, LANES), jnp.bfloat16)],
        compiler_params=pltpu.CompilerParams(
            dimension_semantics=("parallel", "parallel", "arbitrary"), vmem_limit_bytes=VMEM_LIMIT_BYTES),
        name="sb_attention",
    )(z3, z3, z3, g2(qn_g), g2(kn_g), out_g_row, _suffix_and_total_ones(2 * blk))


ML_SEQS = 2


def _lane_cumsum(x):
    lane = lax.broadcasted_iota(jnp.int32, x.shape, 1)
    shift = 1
    while shift < LANES:
        x = x + jnp.where(lane >= shift, pltpu.roll(x, shift, 1), 0.0)
        shift *= 2
    return x


def _as_column(row):
    return jnp.broadcast_to(row, (LANES, LANES)).T


def _mlstm_body(zq_ref, zk_ref, zv_ref, zo_ref, gt_ref, gb_ref, cwq_ref, cwk_ref, cbq_ref, cbk_ref, og_ref,
                o_ref, c_ref, n_ref, m_ref):
    s_len = zq_ref.shape[1]
    blk = ML_CHUNK
    c_ref[...] = jnp.zeros_like(c_ref)
    n_ref[...] = jnp.zeros_like(n_ref)
    m_ref[...] = jnp.zeros_like(m_ref)
    first = _first_half((blk, LANES))
    row = lax.broadcasted_iota(jnp.int32, (blk, blk), 0)
    col = lax.broadcasted_iota(jnp.int32, (blk, blk), 1)
    causal = col <= row
    same_head = (row >> 6) == (col >> 6)
    first_row = _first_half((1, LANES))
    nt = (((1,), (1,)), ((), ()))

    def chunk_of(sq, ci):
        t0 = pl.multiple_of(ci * blk, blk)
        tp = pl.multiple_of(jnp.maximum(t0 - SUBLANES, 0), SUBLANES)

        def conv_silu(z_ref, w_ref, b_ref):
            prev = jnp.where(ci > 0, z_ref[sq, pl.ds(tp, SUBLANES), :], 0.0)
            xx = jnp.concatenate([prev, z_ref[sq, pl.ds(t0, blk), :]], axis=0)
            out = b_ref[...]
            for j in range(CONV_WIDTH):
                lag = SUBLANES - (CONV_WIDTH - 1) + j
                out = out + xx[lag:lag + blk, :] * w_ref[j:j + 1, :]
            return out * jax.nn.sigmoid(out)

        q = conv_silu(zq_ref, cwq_ref, cbq_ref)
        k = conv_silu(zk_ref, cwk_ref, cbk_ref) * (HEAD_DIM ** -0.5)
        qb = q.astype(jnp.bfloat16)
        kb = k.astype(jnp.bfloat16)
        vb = zv_ref[sq, pl.ds(t0, blk), :].astype(jnp.bfloat16)

        g = gt_ref[sq * (s_len // blk) + ci, 0] + gb_ref[0]
        bcum = _lane_cumsum(-_softplus(-g))[2:4]
        ig = g[0:2]
        b_last = bcum[:, blk - 1:blk]
        gs = b_last - bcum + ig
        m_old = m_ref[sq, 0:2, 0:1]
        m_new = jnp.maximum(b_last + m_old, jnp.max(gs, axis=-1, keepdims=True))
        w_row = jnp.exp(gs - m_new)
        decay = jnp.exp(b_last + m_old - m_new)
        rowterm = ig - bcum

        nums, dens, mts, inters = [], [], [], []
        for h in range(2):
            qh = jnp.where(first if h == 0 else jnp.logical_not(first), q, 0.0).astype(jnp.bfloat16)
            qk = lax.dot_general(qh, kb, nt, preferred_element_type=jnp.float32)
            bcol = _as_column(bcum[h:h + 1])
            dmat = jnp.where(causal, bcol + rowterm[h:h + 1], -jnp.inf)
            m_int = bcol[:, 0:1] + m_old[h:h + 1]
            m_t = jnp.maximum(m_int, jnp.max(dmat, axis=-1, keepdims=True))
            pm = qk * jnp.exp(dmat - m_t)
            nums.append(jnp.dot(pm.astype(jnp.bfloat16), vb, preferred_element_type=jnp.float32))
            dens.append(jnp.sum(pm, axis=-1, keepdims=True))
            mts.append(m_t)
            inters.append(jnp.exp(m_int - m_t))

        pick = lambda ab: jnp.where(first, ab[0], ab[1])
        inter = pick(inters)
        qc = jnp.dot(qb, c_ref[sq].astype(jnp.bfloat16), preferred_element_type=jnp.float32)
        qn = q * n_ref[sq, 0:1, :]
        qn = jnp.where(first, jnp.sum(jnp.where(first, qn, 0.0), axis=-1, keepdims=True),
                       jnp.sum(jnp.where(first, 0.0, qn), axis=-1, keepdims=True))
        den = pick(dens) + inter * qn
        hout = (pick(nums) + inter * qc) / jnp.maximum(jnp.abs(den), jnp.exp(-pick(mts)))
        y = _pair_rms(hout, og_ref[...]) * jax.nn.sigmoid(zo_ref[sq, pl.ds(t0, blk), :])
        o_ref[sq, pl.ds(t0, blk), :] = y.astype(o_ref.dtype)

        kw = k * jnp.where(first, _as_column(w_row[0:1]), _as_column(w_row[1:2]))
        upd = jnp.dot(kw.T.astype(jnp.bfloat16), vb, preferred_element_type=jnp.float32)
        dec = jnp.where(first_row, decay[0:1], decay[1:2])
        c_ref[sq] = dec * c_ref[sq] + jnp.where(same_head, upd, 0.0)
        n_ref[sq, 0:1, :] = dec * n_ref[sq, 0:1, :] + jnp.sum(kw, axis=0, keepdims=True)
        m_ref[sq, 0:2, :] = jnp.broadcast_to(m_new, (2, LANES))

    def chunk(ci, carry):
        for sq in range(zq_ref.shape[0]):
            chunk_of(sq, ci)
        return carry

    lax.fori_loop(0, s_len // blk, chunk, 0)


def mlstm(z3, gates_t, ml_i_b, ml_f_b, conv_w, conv_b, out_g_row):
    b, s_len, _ = z3.shape
    pairs = ML_HEADS // 2
    blk = ML_CHUNK
    nc = s_len // blk
    gi = gates_t[:ML_HEADS].reshape(pairs, 2, b * nc, blk)
    gf = gates_t[ML_HEADS:2 * ML_HEADS].reshape(pairs, 2, b * nc, blk)
    gt = jnp.concatenate([gi, gf, jnp.zeros((pairs, SUBLANES - 4, b * nc, blk), jnp.float32)], axis=1)
    gt = gt.transpose(2, 0, 1, 3)
    gb = jnp.concatenate([ml_i_b.reshape(pairs, 2), ml_f_b.reshape(pairs, 2),
                          jnp.zeros((pairs, SUBLANES - 4), jnp.float32)], axis=1).reshape(pairs, SUBLANES, 1)
    nsq = ML_SEQS if b % ML_SEQS == 0 else 1
    zspec = lambda off: pl.BlockSpec((nsq, s_len, LANES), lambda bi, p: (bi, 0, off + p))
    return pl.pallas_call(
        _mlstm_body,
        grid=(b // nsq, pairs),
        in_specs=[
            zspec(9), zspec(12), zspec(15), zspec(18),
            pl.BlockSpec((nsq * nc, 1, SUBLANES, blk), lambda bi, p: (bi, p, 0, 0)),
            pl.BlockSpec((1, SUBLANES, 1), lambda bi, p: (p, 0, 0)),
            pl.BlockSpec((CONV_WIDTH, LANES), lambda bi, p: (0, p)),
            pl.BlockSpec((CONV_WIDTH, LANES), lambda bi, p: (0, pairs + p)),
            pl.BlockSpec((1, LANES), lambda bi, p: (0, p)),
            pl.BlockSpec((1, LANES), lambda bi, p: (0, pairs + p)),
            pl.BlockSpec((1, LANES), lambda bi, p: (0, SB_HEADS // 2 + p)),
        ],
        out_specs=pl.BlockSpec((nsq, s_len, LANES), lambda bi, p: (bi, 0, p)),
        out_shape=jax.ShapeDtypeStruct((b, s_len, ML_WIDTH), jnp.bfloat16),
        scratch_shapes=[pltpu.VMEM((nsq, LANES, LANES), jnp.float32),
                        pltpu.VMEM((nsq, SUBLANES, LANES), jnp.float32),
                        pltpu.VMEM((nsq, SUBLANES, LANES), jnp.float32)],
        compiler_params=pltpu.CompilerParams(
            dimension_semantics=("parallel", "parallel"), vmem_limit_bytes=VMEM_LIMIT_BYTES),
        name="mlstm",
    )(z3, z3, z3, z3, gt, gb, conv_w, conv_w, conv_b.reshape(1, -1), conv_b.reshape(1, -1), out_g_row)


def _sg_body(u_ref, v_ref, w_ref, bias_ref, vg_ref, og_ref, o_ref):
    blk = SG_CHUNK
    first = _first_half((blk, LANES))
    row = lax.broadcasted_iota(jnp.int32, (blk, blk), 0)
    col = lax.broadcasted_iota(jnp.int32, (blk, blk), 1)
    u = jax.nn.gelu(u_ref[0])
    v = _pair_rms(jax.nn.gelu(v_ref[0]), vg_ref[...]).astype(jnp.bfloat16)
    gates = [jnp.dot(jnp.where(col <= row, w_ref[g], 0.0).astype(jnp.bfloat16), v,
                     preferred_element_type=jnp.float32) for g in range(2)]
    y = u * (jnp.where(first, gates[0], gates[1]) + bias_ref[0])
    o_ref[0] = _pair_rms(y, og_ref[...]).astype(o_ref.dtype)


def spatial_gating(z3, sg_vn_g, sg_w, sg_b, out_g_row):
    b, s_len, _ = z3.shape
    pairs = SG_GROUPS // 2
    blk = SG_CHUNK
    bias = jnp.repeat(sg_b.astype(jnp.float32).reshape(pairs, 2, blk).transpose(0, 2, 1), HEAD_DIM, axis=2)
    return pl.pallas_call(
        _sg_body,
        grid=(b, s_len // blk, pairs),
        in_specs=[
            pl.BlockSpec((1, blk, LANES), lambda bi, ci, p: (bi, ci, 21 + p)),
            pl.BlockSpec((1, blk, LANES), lambda bi, ci, p: (bi, ci, 23 + p)),
            pl.BlockSpec((2, blk, blk), lambda bi, ci, p: (p, 0, 0)),
            pl.BlockSpec((1, blk, LANES), lambda bi, ci, p: (p, 0, 0)),
            pl.BlockSpec((1, LANES), lambda bi, ci, p: (0, p)),
            pl.BlockSpec((1, LANES), lambda bi, ci, p: (0, (SB_HEADS + ML_HEADS) // 2 + p)),
        ],
        out_specs=pl.BlockSpec((1, blk, LANES), lambda bi, ci, p: (bi, ci, p)),
        out_shape=jax.ShapeDtypeStruct((b, s_len, SG_WIDTH), jnp.bfloat16),
        compiler_params=pltpu.CompilerParams(
            dimension_semantics=("parallel", "parallel", "parallel"), vmem_limit_bytes=VMEM_LIMIT_BYTES),
        name="spatial_gating",
    )(z3, z3, sg_w.astype(jnp.float32), bias, sg_vn_g.reshape(1, SG_WIDTH).astype(jnp.float32), out_g_row)


def _out_proj_body(x_ref, a_ref, b_ref, c_ref, w_ref, o_ref):
    acc = x_ref[...]
    acc = acc + jnp.dot(a_ref[...], w_ref[0:SB_WIDTH, :], preferred_element_type=jnp.float32)
    acc = acc + jnp.dot(b_ref[...], w_ref[SB_WIDTH:SB_WIDTH + ML_WIDTH, :], preferred_element_type=jnp.float32)
    acc = acc + jnp.dot(c_ref[...], w_ref[SB_WIDTH + ML_WIDTH:, :], preferred_element_type=jnp.float32)
    o_ref[...] = acc


def out_proj(x, y_sb, y_ml, y_sg, w_bf, tm=512):
    t, d = x.shape
    return pl.pallas_call(
        _out_proj_body,
        grid=(t // tm,),
        in_specs=[
            pl.BlockSpec((tm, d), lambda i: (i, 0)),
            pl.BlockSpec((tm, SB_WIDTH), lambda i: (i, 0)),
            pl.BlockSpec((tm, ML_WIDTH), lambda i: (i, 0)),
            pl.BlockSpec((tm, SG_WIDTH), lambda i: (i, 0)),
            pl.BlockSpec((MIX_WIDTH, d), lambda i: (0, 0)),
        ],
        out_specs=pl.BlockSpec((tm, d), lambda i: (i, 0)),
        out_shape=jax.ShapeDtypeStruct((t, d), jnp.float32),
        compiler_params=pltpu.CompilerParams(
            dimension_semantics=("parallel",), vmem_limit_bytes=VMEM_LIMIT_BYTES),
        name="out_proj",
    )(x, y_sb, y_ml, y_sg, w_bf)


def mixing_sublayer(x, norm_g, w_in, sb_qn_g, sb_kn_g, ml_conv_w, ml_conv_b, ml_i_b, ml_f_b,
                    sg_vn_g, sg_w, sg_b, out_g, w_out):
    b, s_len, d = x.shape
    t = b * s_len
    gate0 = sum(IN_SIZES[:6])
    gate1 = gate0 + 2 * ML_HEADS
    w_main = jnp.concatenate([w_in[:, :gate0], w_in[:, gate1:]], axis=1).astype(jnp.bfloat16)
    w_gate_t = jnp.pad(w_in[:, gate0:gate1].T, ((0, GATE_ROWS - 2 * ML_HEADS), (0, 0))).astype(jnp.bfloat16)
    z, gates_t = in_proj(x.reshape(t, d), norm_g, w_main, w_gate_t)
    z3 = z.reshape(b, s_len, Z_WIDTH)
    out_g_row = out_g.reshape(1, MIX_WIDTH).astype(jnp.float32)
    y_sb = sb_attention(z3, sb_qn_g, sb_kn_g, out_g_row)
    y_ml = mlstm(z3, gates_t, ml_i_b.astype(jnp.float32), ml_f_b.astype(jnp.float32),
                 ml_conv_w.astype(jnp.float32), ml_conv_b.astype(jnp.float32), out_g_row)
    y_sg = spatial_gating(z3, sg_vn_g, sg_w, sg_b, out_g_row)
    out = out_proj(x.reshape(t, d), y_sb.reshape(t, SB_WIDTH), y_ml.reshape(t, ML_WIDTH),
                   y_sg.reshape(t, SG_WIDTH), w_out.astype(jnp.bfloat16))
    return out.reshape(b, s_len, d)


NEG_INF = float("-inf")


def _top16_rows(s):
    n, tm = s.shape
    pos = lax.broadcasted_iota(jnp.int32, (n, tm), 0)
    slot = lax.broadcasted_iota(jnp.int32, (PEER_TOPK, tm), 0)
    vals = jnp.zeros((PEER_TOPK, tm), jnp.float32)
    picks = jnp.zeros((PEER_TOPK, tm), jnp.int32)
    for r in range(PEER_TOPK):
        m = jnp.max(s, axis=0, keepdims=True)
        at = jnp.min(jnp.where(s == m, pos, n), axis=0, keepdims=True)
        vals = jnp.where(slot == r, m, vals)
        picks = jnp.where(slot == r, at, picks)
        s = jnp.where(pos == at, NEG_INF, s)
    return vals, picks


def _best_sums(sv0, sv1, si0, si1):
    tm = sv0.shape[1]
    rows = lax.broadcasted_iota(jnp.int32, (PEER_TOPK, tm), 0)
    used = jnp.zeros((PEER_TOPK, tm), jnp.int32)
    front = sv0 + sv1[0:1]
    best = jnp.zeros((PEER_TOPK, tm), jnp.float32)
    pick_a = jnp.zeros((PEER_TOPK, tm), jnp.int32)
    pick_b = jnp.zeros((PEER_TOPK, tm), jnp.int32)
    for r in range(PEER_TOPK):
        m = jnp.max(front, axis=0, keepdims=True)
        a = jnp.min(jnp.where(front == m, rows, PEER_TOPK), axis=0, keepdims=True)
        hit = rows == a
        b = jnp.sum(jnp.where(hit, used, 0), axis=0, keepdims=True)
        best = jnp.where(rows == r, m, best)
        pick_a = jnp.where(rows == r, a, pick_a)
        pick_b = jnp.where(rows == r, b, pick_b)
        used = used + hit.astype(jnp.int32)
        nxt = jnp.sum(jnp.where(rows == b + 1, sv1, 0.0), axis=0, keepdims=True)
        front = jnp.where(hit, jnp.where(b + 1 < PEER_TOPK, sv0 + nxt, NEG_INF), front)
    hi = jnp.zeros((PEER_TOPK, tm), jnp.float32)
    lo = jnp.zeros((PEER_TOPK, tm), jnp.float32)
    for a in range(PEER_TOPK):
        hi = jnp.where(pick_a == a, si0[a:a + 1], hi)
        lo = jnp.where(pick_b == a, si1[a:a + 1], lo)
    return best, hi * PEER_NKEYS + lo


def _peer_route_body(x_ref, g_ref, wq_ref, keys_ref, xn_ref, idx_ref, gate_ref,
                     sc_ref, sv_ref, si_ref, oi_ref, og_ref):
    tm = x_ref.shape[0]
    x = x_ref[...]
    xn = (x * lax.rsqrt(jnp.mean(x * x, axis=-1, keepdims=True) + NORM_EPS) * g_ref[...]).astype(jnp.bfloat16)
    xn_ref[...] = xn
    q = jnp.dot(xn, wq_ref[...], preferred_element_type=jnp.float32).astype(jnp.bfloat16)
    nt = (((1,), (1,)), ((), ()))
    for hc in range(2 * PEER_HEADS):
        sc_ref[hc] = lax.dot_general(keys_ref[hc], q[:, hc * PEER_SUBDIM:(hc + 1) * PEER_SUBDIM], nt,
                                     preferred_element_type=jnp.float32)

    def first_stage(hc, carry):
        vals, picks = _top16_rows(sc_ref[hc])
        sv_ref[hc] = vals
        si_ref[hc] = picks.astype(jnp.float32)
        return carry

    lax.fori_loop(0, 2 * PEER_HEADS, first_stage, 0)

    def second_stage(h, carry):
        best, ids = _best_sums(sv_ref[2 * h], sv_ref[2 * h + 1], si_ref[2 * h], si_ref[2 * h + 1])
        e = jnp.exp(best - best[0:1])
        r0 = pl.multiple_of(h * PEER_TOPK, PEER_TOPK)
        og_ref[pl.ds(r0, PEER_TOPK), :] = e / jnp.sum(e, axis=0, keepdims=True)
        oi_ref[pl.ds(r0, PEER_TOPK), :] = ids
        return carry

    lax.fori_loop(0, PEER_HEADS, second_stage, 0)
    idx_ref[...] = oi_ref[...].T.astype(jnp.int32)
    gate_ref[...] = og_ref[...].T


def peer_route(x, g, wq_bf, keys_bf, tm=512):
    t, d = x.shape
    nq = wq_bf.shape[1]
    hc = 2 * PEER_HEADS
    return pl.pallas_call(
        _peer_route_body,
        grid=(t // tm,),
        in_specs=[
            pl.BlockSpec((tm, d), lambda i: (i, 0)),
            pl.BlockSpec((1, d), lambda i: (0, 0)),
            pl.BlockSpec((d, nq), lambda i: (0, 0)),
            pl.BlockSpec((hc, PEER_NKEYS, PEER_SUBDIM), lambda i: (0, 0, 0)),
        ],
        out_specs=[pl.BlockSpec((tm, d), lambda i: (i, 0)),
                   pl.BlockSpec((tm, PAIRS), lambda i: (i, 0)),
                   pl.BlockSpec((tm, PAIRS), lambda i: (i, 0))],
        out_shape=[jax.ShapeDtypeStruct((t, d), jnp.bfloat16),
                   jax.ShapeDtypeStruct((t, PAIRS), jnp.int32),
                   jax.ShapeDtypeStruct((t, PAIRS), jnp.float32)],
        scratch_shapes=[pltpu.VMEM((hc, PEER_NKEYS, tm), jnp.float32),
                        pltpu.VMEM((hc, PEER_TOPK, tm), jnp.float32),
                        pltpu.VMEM((hc, PEER_TOPK, tm), jnp.float32),
                        pltpu.VMEM((PAIRS, tm), jnp.float32),
                        pltpu.VMEM((PAIRS, tm), jnp.float32)],
        compiler_params=pltpu.CompilerParams(
            dimension_semantics=("parallel",), vmem_limit_bytes=VMEM_LIMIT_BYTES),
        name="peer_route",
    )(x, g.reshape(1, d), wq_bf, keys_bf.reshape(hc, PEER_NKEYS, PEER_SUBDIM))


def peer_ffn(x, norm_g, wq, sub_keys, u_tab, v_tab):
    b, s_len, d = x.shape
    t = b * s_len
    x2 = x.reshape(t, d)
    xn_bf, idx, gate = peer_route(x2, norm_g, wq.astype(jnp.bfloat16), sub_keys.astype(jnp.bfloat16))
    wgt = peer_act(xn_bf, u_tab.astype(jnp.bfloat16), idx, gate)
    return peer_out(wgt, idx, v_tab.astype(jnp.bfloat16), x2).reshape(b, s_len, d)


def kernel(x, norm1_g, w_in, sb_qn_g, sb_kn_g, ml_conv_w, ml_conv_b, ml_i_b, ml_f_b, sg_vn_g, sg_w, sg_b,
           out_g, w_out, norm2_g, peer_wq, peer_keys, peer_u, peer_v):
    for l in range(DEPTH):
        x = mixing_sublayer(x, norm1_g[l], w_in[l], sb_qn_g[l], sb_kn_g[l], ml_conv_w[l], ml_conv_b[l],
                            ml_i_b[l], ml_f_b[l], sg_vn_g[l], sg_w[l], sg_b[l], out_g[l], w_out[l])
        x = peer_ffn(x, norm2_g[l], peer_wq[l], peer_keys[l], peer_u[l], peer_v[l])
    return x
```

```python
import functools

import jax
import jax.numpy as jnp
import numpy as np
from jax import lax
from jax.experimental import pallas as pl
from jax.experimental.pallas import tpu as pltpu

D_MODEL = 1024
DEPTH = 4
HEAD_DIM = 64
SB_HEADS = 6
ML_HEADS = 6
SG_GROUPS = 4
SB_WIDTH = SB_HEADS * HEAD_DIM
ML_WIDTH = ML_HEADS * HEAD_DIM
SG_WIDTH = SG_GROUPS * HEAD_DIM
MIX_HEADS = SB_HEADS + ML_HEADS + SG_GROUPS
MIX_WIDTH = SB_WIDTH + ML_WIDTH + SG_WIDTH
IN_SIZES = (SB_WIDTH, SB_WIDTH, SB_WIDTH, 2 * ML_WIDTH, ML_WIDTH, ML_WIDTH, ML_HEADS, ML_HEADS, 2 * SG_WIDTH)
IN_WIDTH = sum(IN_SIZES)
Q_BLOCK = 128
ML_CHUNK = 128
SG_CHUNK = 128
CONV_WIDTH = 4
PEER_HEADS = 8
PEER_NKEYS = 128
PEER_EXPERTS = PEER_NKEYS * PEER_NKEYS
PEER_SUBDIM = 128
PEER_TOPK = 16
PEER_BLOCK = 128
NORM_EPS = 1e-6

VMEM_LIMIT_BYTES = 56 * 1024 * 1024


LANES = 128
PAIRS = PEER_HEADS * PEER_TOPK
MXU_K = 256


def _peer_act_body(xn_ref, u_ref, idx_ref, gate_ref, o_ref, sel_ref, *, groups):
    ei = pl.program_id(1)

    @pl.when(ei == 0)
    def _():
        sel_ref[...] = jnp.zeros_like(sel_ref)

    idx = idx_ref[...]
    hi = idx >> 7
    lo = idx & (LANES - 1)
    sel = sel_ref[...]
    xn = xn_ref[...]
    for gg in range(groups * LANES // MXU_K):
        act = lax.dot_general(xn, u_ref[gg * MXU_K:(gg + 1) * MXU_K, :], (((1,), (1,)), ((), ())),
                              preferred_element_type=jnp.float32)
        for h in range(MXU_K // LANES):
            g = gg * (MXU_K // LANES) + h
            picked = jnp.take_along_axis(act[:, h * LANES:(h + 1) * LANES], lo, axis=1)
            sel = jnp.where(hi == ei * groups + g, picked, sel)
    sel_ref[...] = sel

    @pl.when(ei == pl.num_programs(1) - 1)
    def _():
        o_ref[...] = jax.nn.gelu(sel_ref[...]) * gate_ref[...]


def peer_act(xn_bf, u_bf, idx, gate, tm=512, te=4096):
    t, d = xn_bf.shape
    e = u_bf.shape[0]
    return pl.pallas_call(
        functools.partial(_peer_act_body, groups=te // LANES),
        grid=(t // tm, e // te),
        in_specs=[
            pl.BlockSpec((tm, d), lambda i, j: (i, 0)),
            pl.BlockSpec((te, d), lambda i, j: (j, 0)),
            pl.BlockSpec((tm, PAIRS), lambda i, j: (i, 0)),
            pl.BlockSpec((tm, PAIRS), lambda i, j: (i, 0)),
        ],
        out_specs=pl.BlockSpec((tm, PAIRS), lambda i, j: (i, 0)),
        out_shape=jax.ShapeDtypeStruct((t, PAIRS), jnp.float32),
        scratch_shapes=[pltpu.VMEM((tm, PAIRS), jnp.float32)],
        compiler_params=pltpu.CompilerParams(
            dimension_semantics=("parallel", "arbitrary"), vmem_limit_bytes=VMEM_LIMIT_BYTES),
        name="peer_act",
    )(xn_bf, u_bf, idx, gate)


SUBLANES = 8
SCATTER_TOKENS = 64
STAGE_PITCH = LANES + SUBLANES
REGROUP_UNROLL = 4


def _peer_out_body(wgt_ref, idx_ref, v_ref, x_ref, o_ref, w_ref, stage_ref, *, kgroups):
    ei = pl.program_id(1)
    tm = wgt_ref.shape[0]

    @pl.when(ei == 0)
    def _():
        sub_iota = lax.broadcasted_iota(jnp.int32, (LANES, PAIRS), 0).astype(jnp.float32).astype(jnp.bfloat16)
        one = jnp.ones((), jnp.bfloat16)
        zero = jnp.zeros((), jnp.bfloat16)

        def block(sb, carry):
            t0 = pl.multiple_of(sb * SCATTER_TOKENS, SCATTER_TOKENS)

            as_bf = lambda v: v.astype(jnp.float32).astype(jnp.bfloat16)
            for tb in range(SCATTER_TOKENS // SUBLANES):
                r0 = pl.multiple_of(t0 + tb * SUBLANES, SUBLANES)
                idx8 = idx_ref[pl.ds(r0, SUBLANES), :]
                hi8 = as_bf(idx8 >> 7)
                lo8 = as_bf(idx8 & (LANES - 1))
                w8 = wgt_ref[pl.ds(r0, SUBLANES), :].astype(jnp.bfloat16)
                for s in range(SUBLANES):
                    a = jnp.where(sub_iota == hi8[s:s + 1, :], one, zero)
                    b = jnp.where(sub_iota == lo8[s:s + 1, :], w8[s:s + 1, :], zero)
                    stage_ref[pl.ds((tb * SUBLANES + s) * STAGE_PITCH, LANES), :] = lax.dot_general(
                        a, b, (((1,), (1,)), ((), ())), preferred_element_type=jnp.float32)

            def regroup(ig, c2):
                for u in range(REGROUP_UNROLL):
                    ip = ig * REGROUP_UNROLL + u
                    for half in range(MXU_K // LANES):
                        rows = stage_ref[pl.ds(ip * (MXU_K // LANES) + half, SCATTER_TOKENS, stride=STAGE_PITCH), :]
                        w_ref[ip, pl.ds(t0, SCATTER_TOKENS), half * LANES:(half + 1) * LANES] = (
                            rows.astype(jnp.bfloat16))
                return c2

            lax.fori_loop(0, LANES * LANES // MXU_K // REGROUP_UNROLL, regroup, 0)
            return carry

        lax.fori_loop(0, tm // SCATTER_TOKENS, block, 0)
        o_ref[...] = x_ref[...]

    acc = o_ref[...]
    for g in range(kgroups):
        acc = acc + jnp.dot(w_ref[ei * kgroups + g], v_ref[g * MXU_K:(g + 1) * MXU_K, :],
                            preferred_element_type=jnp.float32)
    o_ref[...] = acc


def peer_out(wgt, idx, v_bf, x, tm=512, te=4096):
    t, d = x.shape
    e = v_bf.shape[0]
    return pl.pallas_call(
        functools.partial(_peer_out_body, kgroups=te // MXU_K),
        grid=(t // tm, e // te),
        in_specs=[
            pl.BlockSpec((tm, PAIRS), lambda i, j: (i, 0)),
            pl.BlockSpec((tm, PAIRS), lambda i, j: (i, 0)),
            pl.BlockSpec((te, d), lambda i, j: (j, 0)),
            pl.BlockSpec((tm, d), lambda i, j: (i, 0)),
        ],
        out_specs=pl.BlockSpec((tm, d), lambda i, j: (i, 0)),
        out_shape=jax.ShapeDtypeStruct((t, d), jnp.float32),
        scratch_shapes=[pltpu.VMEM((e // MXU_K, tm, MXU_K), jnp.bfloat16),
                        pltpu.VMEM((SCATTER_TOKENS * STAGE_PITCH, LANES), jnp.float32)],
        compiler_params=pltpu.CompilerParams(
            dimension_semantics=("parallel", "arbitrary"), vmem_limit_bytes=VMEM_LIMIT_BYTES),
        name="peer_out",
    )(wgt, idx, v_bf, x)


Z_BLOCKS = 25
Z_WIDTH = Z_BLOCKS * LANES
GATE_ROWS = 16


def _first_half(shape):
    return lax.broadcasted_iota(jnp.int32, shape, len(shape) - 1) < HEAD_DIM


def _pair_rms(x, g):
    first = _first_half(x.shape)
    x2 = x * x
    s_a = jnp.sum(jnp.where(first, x2, 0.0), axis=-1, keepdims=True)
    s_b = jnp.sum(jnp.where(first, 0.0, x2), axis=-1, keepdims=True)
    return x * lax.rsqrt(jnp.where(first, s_a, s_b) * (1.0 / HEAD_DIM) + NORM_EPS) * g


def _softplus(z):
    return jnp.maximum(z, 0.0) + jnp.log(1.0 + jnp.exp(-jnp.abs(z)))


def _in_proj_body(x_ref, g_ref, w_ref, wg_ref, z_ref, gt_ref):
    x = x_ref[...]
    xn = (x * lax.rsqrt(jnp.mean(x * x, axis=-1, keepdims=True) + NORM_EPS) * g_ref[...]).astype(jnp.bfloat16)
    z_ref[...] = jnp.dot(xn, w_ref[...], preferred_element_type=jnp.float32)
    gt_ref[...] = lax.dot_general(wg_ref[...], xn, (((1,), (1,)), ((), ())), preferred_element_type=jnp.float32)


def in_proj(x, g, w_main, w_gate_t, tm=512):
    t, d = x.shape
    return pl.pallas_call(
        _in_proj_body,
        grid=(t // tm,),
        in_specs=[
            pl.BlockSpec((tm, d), lambda i: (i, 0)),
            pl.BlockSpec((1, d), lambda i: (0, 0)),
            pl.BlockSpec((d, Z_WIDTH), lambda i: (0, 0)),
            pl.BlockSpec((GATE_ROWS, d), lambda i: (0, 0)),
        ],
        out_specs=[pl.BlockSpec((tm, Z_WIDTH), lambda i: (i, 0)),
                   pl.BlockSpec((GATE_ROWS, tm), lambda i: (0, i))],
        out_shape=[jax.ShapeDtypeStruct((t, Z_WIDTH), jnp.float32),
                   jax.ShapeDtypeStruct((GATE_ROWS, t), jnp.float32)],
        compiler_params=pltpu.CompilerParams(
            dimension_semantics=("parallel",), vmem_limit_bytes=VMEM_LIMIT_BYTES),
        name="in_proj",
    )(x, g.reshape(1, d), w_main, w_gate_t)


ATTN_BLOCK = 256
PREP_ROWS = 256
EXP_ZERO_LOG = -104.0


def _sb_attn_body(q_ref, k_ref, v_ref, qg_ref, kg_ref, og_ref, mcat_ref, o_ref, kn_ref, vb_ref):
    qi = pl.program_id(2)
    blk = ATTN_BLOCK
    s_len = k_ref.shape[1]

    @pl.when(qi == 0)
    def _():
        def prep(r, carry):
            r0 = pl.multiple_of(r * PREP_ROWS, PREP_ROWS)
            kn_ref[pl.ds(r0, PREP_ROWS), :] = _pair_rms(k_ref[0, pl.ds(r0, PREP_ROWS), :],
                                                        kg_ref[...]).astype(jnp.bfloat16)
            vb_ref[pl.ds(r0, PREP_ROWS), :] = v_ref[0, pl.ds(r0, PREP_ROWS), :].astype(jnp.bfloat16)
            return carry

        lax.fori_loop(0, s_len // PREP_ROWS, prep, 0)

    qn = _pair_rms(q_ref[0], qg_ref[...]) * (HEAD_DIM ** -0.5)
    first = _first_half(qn.shape)
    qh = jnp.concatenate([jnp.where(first, qn, 0.0), jnp.where(first, 0.0, qn)], axis=0).astype(jnp.bfloat16)
    strict = (lax.broadcasted_iota(jnp.int32, (2 * blk, blk), 1)
              < (lax.broadcasted_iota(jnp.int32, (2 * blk, blk), 0) & (blk - 1)))
    mcat = mcat_ref[...]
    nt = (((1,), (1,)), ((), ()))

    def key_block(j, c, acc, diagonal):
        k0 = pl.multiple_of(j * blk, blk)
        z = lax.dot_general(qh, kn_ref[pl.ds(k0, blk), :], nt, preferred_element_type=jnp.float32)
        lk = -_softplus(z)
        if diagonal:
            lk = jnp.where(strict, lk, 0.0)
        hi = lk.astype(jnp.bfloat16)
        lo = (lk - hi.astype(jnp.float32)).astype(jnp.bfloat16)
        ct = (jnp.dot(hi, mcat, preferred_element_type=jnp.float32)
              + jnp.dot(lo, mcat, preferred_element_type=jnp.float32))
        w = jnp.exp(z + c + ct[:, :blk])
        if diagonal:
            w = jnp.where(strict, w, 0.0)
        acc = acc + jnp.dot(w.astype(jnp.bfloat16), vb_ref[pl.ds(k0, blk), :], preferred_element_type=jnp.float32)
        return c + ct[:, blk:], acc

    c, acc = key_block(qi, jnp.zeros((2 * blk, blk), jnp.float32), jnp.zeros((2 * blk, LANES), jnp.float32), True)

    def cond(carry):
        j, go, _, _ = carry
        return jnp.logical_and(j >= 0, go > 0)

    def body(carry):
        j, _, c, acc = carry
        c, acc = key_block(j, c, acc, False)
        return j - 1, (jnp.max(c) > EXP_ZERO_LOG).astype(jnp.int32), c, acc

    _, _, _, acc = lax.while_loop(cond, body, (qi - 1, jnp.int32(1), c, acc))
    y = jnp.where(first, acc[:blk], acc[blk:])
    o_ref[0] = _pair_rms(y, og_ref[...]).astype(o_ref.dtype)


def _suffix_and_total_ones(n):
    r = np.arange(n)
    suffix = (r[:, None] >= r[None, :]).astype(np.float32)
    return jnp.asarray(np.concatenate([suffix, np.ones((n, n), np.float32)], axis=1), jnp.bfloat16)


def sb_attention(z3, qn_g, kn_g, out_g_row):
    b, s_len, _ = z3.shape
    pairs = SB_HEADS // 2
    blk = ATTN_BLOCK
    g2 = lambda g: jnp.concatenate([g, g]).reshape(1, LANES).astype(jnp.float32)
    return pl.pallas_call(
        _sb_attn_body,
        grid=(b, pairs, s_len // blk),
        in_specs=[
            pl.BlockSpec((1, blk, LANES), lambda bi, p, qi: (bi, qi, p)),
            pl.BlockSpec((1, s_len, LANES), lambda bi, p, qi: (bi, 0, pairs + p)),
            pl.BlockSpec((1, s_len, LANES), lambda bi, p, qi: (bi, 0, 2 * pairs + p)),
            pl.BlockSpec((1, LANES), lambda bi, p, qi: (0, 0)),
            pl.BlockSpec((1, LANES), lambda bi, p, qi: (0, 0)),
            pl.BlockSpec((1, LANES), lambda bi, p, qi: (0, p)),
            pl.BlockSpec((blk, 2 * blk), lambda bi, p, qi: (0, 0)),
        ],
        out_specs=pl.BlockSpec((1, blk, LANES), lambda bi, p, qi: (bi, qi, p)),
        out_shape=jax.ShapeDtypeStruct((b, s_len, SB_WIDTH), jnp.bfloat16),
        scratch_shapes=[pltpu.VMEM((s_len, LANES), jnp.bfloat16), pltpu.VMEM((s_len, LANES), jnp.bfloat16)],
        compiler_params=pltpu.CompilerParams(
            dimension_semantics=("parallel", "parallel", "arbitrary"), vmem_limit_bytes=VMEM_LIMIT_BYTES),
        name="sb_attention",
    )(z3, z3, z3, g2(qn_g), g2(kn_g), out_g_row, _suffix_and_total_ones(blk))


ML_SEQS = 2


def _lane_cumsum(x):
    lane = lax.broadcasted_iota(jnp.int32, x.shape, 1)
    shift = 1
    while shift < LANES:
        x = x + jnp.where(lane >= shift, pltpu.roll(x, shift, 1), 0.0)
        shift *= 2
    return x


def _as_column(row):
    return jnp.broadcast_to(row, (LANES, LANES)).T


def _mlstm_body(zq_ref, zk_ref, zv_ref, zo_ref, gt_ref, gb_ref, cwq_ref, cwk_ref, cbq_ref, cbk_ref, og_ref,
                o_ref, c_ref, n_ref, m_ref):
    s_len = zq_ref.shape[1]
    blk = ML_CHUNK
    c_ref[...] = jnp.zeros_like(c_ref)
    n_ref[...] = jnp.zeros_like(n_ref)
    m_ref[...] = jnp.zeros_like(m_ref)
    first = _first_half((blk, LANES))
    row = lax.broadcasted_iota(jnp.int32, (blk, blk), 0)
    col = lax.broadcasted_iota(jnp.int32, (blk, blk), 1)
    causal = col <= row
    same_head = (row >> 6) == (col >> 6)
    first_row = _first_half((1, LANES))
    nt = (((1,), (1,)), ((), ()))

    def chunk_of(sq, ci):
        t0 = pl.multiple_of(ci * blk, blk)
        tp = pl.multiple_of(jnp.maximum(t0 - SUBLANES, 0), SUBLANES)

        def conv_silu(z_ref, w_ref, b_ref):
            prev = jnp.where(ci > 0, z_ref[sq, pl.ds(tp, SUBLANES), :], 0.0)
            xx = jnp.concatenate([prev, z_ref[sq, pl.ds(t0, blk), :]], axis=0)
            out = b_ref[...]
            for j in range(CONV_WIDTH):
                lag = SUBLANES - (CONV_WIDTH - 1) + j
                out = out + xx[lag:lag + blk, :] * w_ref[j:j + 1, :]
            return out * jax.nn.sigmoid(out)

        q = conv_silu(zq_ref, cwq_ref, cbq_ref)
        k = conv_silu(zk_ref, cwk_ref, cbk_ref) * (HEAD_DIM ** -0.5)
        qb = q.astype(jnp.bfloat16)
        kb = k.astype(jnp.bfloat16)
        vb = zv_ref[sq, pl.ds(t0, blk), :].astype(jnp.bfloat16)

        g = gt_ref[sq * (s_len // blk) + ci, 0] + gb_ref[0]
        bcum = _lane_cumsum(-_softplus(-g))[2:4]
        ig = g[0:2]
        b_last = bcum[:, blk - 1:blk]
        gs = b_last - bcum + ig
        m_old = m_ref[sq, 0:2, 0:1]
        m_new = jnp.maximum(b_last + m_old, jnp.max(gs, axis=-1, keepdims=True))
        w_row = jnp.exp(gs - m_new)
        decay = jnp.exp(b_last + m_old - m_new)
        rowterm = ig - bcum

        nums, dens, mts, inters = [], [], [], []
        for h in range(2):
            qh = jnp.where(first if h == 0 else jnp.logical_not(first), q, 0.0).astype(jnp.bfloat16)
            qk = lax.dot_general(qh, kb, nt, preferred_element_type=jnp.float32)
            bcol = _as_column(bcum[h:h + 1])
            dmat = jnp.where(causal, bcol + rowterm[h:h + 1], -jnp.inf)
            m_int = bcol[:, 0:1] + m_old[h:h + 1]
            m_t = jnp.maximum(m_int, jnp.max(dmat, axis=-1, keepdims=True))
            pm = qk * jnp.exp(dmat - m_t)
            nums.append(jnp.dot(pm.astype(jnp.bfloat16), vb, preferred_element_type=jnp.float32))
            dens.append(jnp.sum(pm, axis=-1, keepdims=True))
            mts.append(m_t)
            inters.append(jnp.exp(m_int - m_t))

        pick = lambda ab: jnp.where(first, ab[0], ab[1])
        inter = pick(inters)
        qc = jnp.dot(qb, c_ref[sq].astype(jnp.bfloat16), preferred_element_type=jnp.float32)
        qn = q * n_ref[sq, 0:1, :]
        qn = jnp.where(first, jnp.sum(jnp.where(first, qn, 0.0), axis=-1, keepdims=True),
                       jnp.sum(jnp.where(first, 0.0, qn), axis=-1, keepdims=True))
        den = pick(dens) + inter * qn
        hout = (pick(nums) + inter * qc) / jnp.maximum(jnp.abs(den), jnp.exp(-pick(mts)))
        y = _pair_rms(hout, og_ref[...]) * jax.nn.sigmoid(zo_ref[sq, pl.ds(t0, blk), :])
        o_ref[sq, pl.ds(t0, blk), :] = y.astype(o_ref.dtype)

        kw = k * jnp.where(first, _as_column(w_row[0:1]), _as_column(w_row[1:2]))
        upd = jnp.dot(kw.T.astype(jnp.bfloat16), vb, preferred_element_type=jnp.float32)
        dec = jnp.where(first_row, decay[0:1], decay[1:2])
        c_ref[sq] = dec * c_ref[sq] + jnp.where(same_head, upd, 0.0)
        n_ref[sq, 0:1, :] = dec * n_ref[sq, 0:1, :] + jnp.sum(kw, axis=0, keepdims=True)
        m_ref[sq, 0:2, :] = jnp.broadcast_to(m_new, (2, LANES))

    def chunk(ci, carry):
        for sq in range(zq_ref.shape[0]):
            chunk_of(sq, ci)
        return carry

    lax.fori_loop(0, s_len // blk, chunk, 0)


def mlstm(z3, gates_t, ml_i_b, ml_f_b, conv_w, conv_b, out_g_row):
    b, s_len, _ = z3.shape
    pairs = ML_HEADS // 2
    blk = ML_CHUNK
    nc = s_len // blk
    gi = gates_t[:ML_HEADS].reshape(pairs, 2, b * nc, blk)
    gf = gates_t[ML_HEADS:2 * ML_HEADS].reshape(pairs, 2, b * nc, blk)
    gt = jnp.concatenate([gi, gf, jnp.zeros((pairs, SUBLANES - 4, b * nc, blk), jnp.float32)], axis=1)
    gt = gt.transpose(2, 0, 1, 3)
    gb = jnp.concatenate([ml_i_b.reshape(pairs, 2), ml_f_b.reshape(pairs, 2),
                          jnp.zeros((pairs, SUBLANES - 4), jnp.float32)], axis=1).reshape(pairs, SUBLANES, 1)
    nsq = ML_SEQS if b % ML_SEQS == 0 else 1
    zspec = lambda off: pl.BlockSpec((nsq, s_len, LANES), lambda bi, p: (bi, 0, off + p))
    return pl.pallas_call(
        _mlstm_body,
        grid=(b // nsq, pairs),
        in_specs=[
            zspec(9), zspec(12), zspec(15), zspec(18),
            pl.BlockSpec((nsq * nc, 1, SUBLANES, blk), lambda bi, p: (bi, p, 0, 0)),
            pl.BlockSpec((1, SUBLANES, 1), lambda bi, p: (p, 0, 0)),
            pl.BlockSpec((CONV_WIDTH, LANES), lambda bi, p: (0, p)),
            pl.BlockSpec((CONV_WIDTH, LANES), lambda bi, p: (0, pairs + p)),
            pl.BlockSpec((1, LANES), lambda bi, p: (0, p)),
            pl.BlockSpec((1, LANES), lambda bi, p: (0, pairs + p)),
            pl.BlockSpec((1, LANES), lambda bi, p: (0, SB_HEADS // 2 + p)),
        ],
        out_specs=pl.BlockSpec((nsq, s_len, LANES), lambda bi, p: (bi, 0, p)),
        out_shape=jax.ShapeDtypeStruct((b, s_len, ML_WIDTH), jnp.bfloat16),
        scratch_shapes=[pltpu.VMEM((nsq, LANES, LANES), jnp.float32),
                        pltpu.VMEM((nsq, SUBLANES, LANES), jnp.float32),
                        pltpu.VMEM((nsq, SUBLANES, LANES), jnp.float32)],
        compiler_params=pltpu.CompilerParams(
            dimension_semantics=("parallel", "parallel"), vmem_limit_bytes=VMEM_LIMIT_BYTES),
        name="mlstm",
    )(z3, z3, z3, z3, gt, gb, conv_w, conv_w, conv_b.reshape(1, -1), conv_b.reshape(1, -1), out_g_row)


SG_CHUNKS_PER_STEP = 8


def _sg_body(u_ref, v_ref, w_ref, bias_ref, vg_ref, og_ref, o_ref):
    blk = SG_CHUNK
    first = _first_half((blk, LANES))
    row = lax.broadcasted_iota(jnp.int32, (blk, blk), 0)
    col = lax.broadcasted_iota(jnp.int32, (blk, blk), 1)
    w_tril = [jnp.where(col <= row, w_ref[g], 0.0).astype(jnp.bfloat16) for g in range(2)]
    for ci in range(u_ref.shape[1] // blk):
        rows = slice(ci * blk, (ci + 1) * blk)
        u = jax.nn.gelu(u_ref[0, rows, :])
        v = _pair_rms(jax.nn.gelu(v_ref[0, rows, :]), vg_ref[...]).astype(jnp.bfloat16)
        gates = [jnp.dot(w_tril[g], v, preferred_element_type=jnp.float32) for g in range(2)]
        y = u * (jnp.where(first, gates[0], gates[1]) + bias_ref[0])
        o_ref[0, rows, :] = _pair_rms(y, og_ref[...]).astype(o_ref.dtype)


def spatial_gating(z3, sg_vn_g, sg_w, sg_b, out_g_row):
    b, s_len, _ = z3.shape
    pairs = SG_GROUPS // 2
    blk = SG_CHUNK
    bias = jnp.repeat(sg_b.astype(jnp.float32).reshape(pairs, 2, blk).transpose(0, 2, 1), HEAD_DIM, axis=2)
    rows = blk * SG_CHUNKS_PER_STEP if s_len % (blk * SG_CHUNKS_PER_STEP) == 0 else blk
    return pl.pallas_call(
        _sg_body,
        grid=(pairs, b, s_len // rows),
        in_specs=[
            pl.BlockSpec((1, rows, LANES), lambda p, bi, ci: (bi, ci, 21 + p)),
            pl.BlockSpec((1, rows, LANES), lambda p, bi, ci: (bi, ci, 23 + p)),
            pl.BlockSpec((2, blk, blk), lambda p, bi, ci: (p, 0, 0)),
            pl.BlockSpec((1, blk, LANES), lambda p, bi, ci: (p, 0, 0)),
            pl.BlockSpec((1, LANES), lambda p, bi, ci: (0, p)),
            pl.BlockSpec((1, LANES), lambda p, bi, ci: (0, (SB_HEADS + ML_HEADS) // 2 + p)),
        ],
        out_specs=pl.BlockSpec((1, rows, LANES), lambda p, bi, ci: (bi, ci, p)),
        out_shape=jax.ShapeDtypeStruct((b, s_len, SG_WIDTH), jnp.bfloat16),
        compiler_params=pltpu.CompilerParams(
            dimension_semantics=("parallel", "parallel", "parallel"), vmem_limit_bytes=VMEM_LIMIT_BYTES),
        name="spatial_gating",
    )(z3, z3, sg_w.astype(jnp.float32), bias, sg_vn_g.reshape(1, SG_WIDTH).astype(jnp.float32), out_g_row)


def _out_proj_body(x_ref, a_ref, b_ref, c_ref, w_ref, o_ref):
    acc = x_ref[...]
    acc = acc + jnp.dot(a_ref[...], w_ref[0:SB_WIDTH, :], preferred_element_type=jnp.float32)
    acc = acc + jnp.dot(b_ref[...], w_ref[SB_WIDTH:SB_WIDTH + ML_WIDTH, :], preferred_element_type=jnp.float32)
    acc = acc + jnp.dot(c_ref[...], w_ref[SB_WIDTH + ML_WIDTH:, :], preferred_element_type=jnp.float32)
    o_ref[...] = acc


def out_proj(x, y_sb, y_ml, y_sg, w_bf, tm=512):
    t, d = x.shape
    return pl.pallas_call(
        _out_proj_body,
        grid=(t // tm,),
        in_specs=[
            pl.BlockSpec((tm, d), lambda i: (i, 0)),
            pl.BlockSpec((tm, SB_WIDTH), lambda i: (i, 0)),
            pl.BlockSpec((tm, ML_WIDTH), lambda i: (i, 0)),
            pl.BlockSpec((tm, SG_WIDTH), lambda i: (i, 0)),
            pl.BlockSpec((MIX_WIDTH, d), lambda i: (0, 0)),
        ],
        out_specs=pl.BlockSpec((tm, d), lambda i: (i, 0)),
        out_shape=jax.ShapeDtypeStruct((t, d), jnp.float32),
        compiler_params=pltpu.CompilerParams(
            dimension_semantics=("parallel",), vmem_limit_bytes=VMEM_LIMIT_BYTES),
        name="out_proj",
    )(x, y_sb, y_ml, y_sg, w_bf)


def mixing_sublayer(x, norm_g, w_in, sb_qn_g, sb_kn_g, ml_conv_w, ml_conv_b, ml_i_b, ml_f_b,
                    sg_vn_g, sg_w, sg_b, out_g, w_out):
    b, s_len, d = x.shape
    t = b * s_len
    gate0 = sum(IN_SIZES[:6])
    gate1 = gate0 + 2 * ML_HEADS
    w_main = jnp.concatenate([w_in[:, :gate0], w_in[:, gate1:]], axis=1).astype(jnp.bfloat16)
    w_gate_t = jnp.pad(w_in[:, gate0:gate1].T, ((0, GATE_ROWS - 2 * ML_HEADS), (0, 0))).astype(jnp.bfloat16)
    z, gates_t = in_proj(x.reshape(t, d), norm_g, w_main, w_gate_t)
    z3 = z.reshape(b, s_len, Z_WIDTH)
    out_g_row = out_g.reshape(1, MIX_WIDTH).astype(jnp.float32)
    y_sb = sb_attention(z3, sb_qn_g, sb_kn_g, out_g_row)
    y_ml = mlstm(z3, gates_t, ml_i_b.astype(jnp.float32), ml_f_b.astype(jnp.float32),
                 ml_conv_w.astype(jnp.float32), ml_conv_b.astype(jnp.float32), out_g_row)
    y_sg = spatial_gating(z3, sg_vn_g, sg_w, sg_b, out_g_row)
    out = out_proj(x.reshape(t, d), y_sb.reshape(t, SB_WIDTH), y_ml.reshape(t, ML_WIDTH),
                   y_sg.reshape(t, SG_WIDTH), w_out.astype(jnp.bfloat16))
    return out.reshape(b, s_len, d)


NEG_INF = float("-inf")


def _top16_rows(s):
    n, tm = s.shape
    pos = lax.broadcasted_iota(jnp.int32, (n, tm), 0)
    slot = lax.broadcasted_iota(jnp.int32, (PEER_TOPK, tm), 0)
    vals = jnp.zeros((PEER_TOPK, tm), jnp.float32)
    picks = jnp.zeros((PEER_TOPK, tm), jnp.int32)
    for r in range(PEER_TOPK):
        m = jnp.max(s, axis=0, keepdims=True)
        at = jnp.min(jnp.where(s == m, pos, n), axis=0, keepdims=True)
        vals = jnp.where(slot == r, m, vals)
        picks = jnp.where(slot == r, at, picks)
        s = jnp.where(pos == at, NEG_INF, s)
    return vals, picks


def _best_sums(sv0, sv1, si0, si1):
    tm = sv0.shape[1]
    rows = lax.broadcasted_iota(jnp.int32, (PEER_TOPK, tm), 0)
    used = jnp.zeros((PEER_TOPK, tm), jnp.int32)
    front = sv0 + sv1[0:1]
    best = jnp.zeros((PEER_TOPK, tm), jnp.float32)
    pick_a = jnp.zeros((PEER_TOPK, tm), jnp.int32)
    pick_b = jnp.zeros((PEER_TOPK, tm), jnp.int32)
    for r in range(PEER_TOPK):
        m = jnp.max(front, axis=0, keepdims=True)
        a = jnp.min(jnp.where(front == m, rows, PEER_TOPK), axis=0, keepdims=True)
        hit = rows == a
        b = jnp.sum(jnp.where(hit, used, 0), axis=0, keepdims=True)
        best = jnp.where(rows == r, m, best)
        pick_a = jnp.where(rows == r, a, pick_a)
        pick_b = jnp.where(rows == r, b, pick_b)
        used = used + hit.astype(jnp.int32)
        nxt = jnp.sum(jnp.where(rows == b + 1, sv1, 0.0), axis=0, keepdims=True)
        front = jnp.where(hit, jnp.where(b + 1 < PEER_TOPK, sv0 + nxt, NEG_INF), front)
    hi = jnp.zeros((PEER_TOPK, tm), jnp.float32)
    lo = jnp.zeros((PEER_TOPK, tm), jnp.float32)
    for a in range(PEER_TOPK):
        hi = jnp.where(pick_a == a, si0[a:a + 1], hi)
        lo = jnp.where(pick_b == a, si1[a:a + 1], lo)
    return best, hi * PEER_NKEYS + lo


def _peer_route_body(x_ref, g_ref, wq_ref, keys_ref, xn_ref, idx_ref, gate_ref,
                     sc_ref, sv_ref, si_ref, oi_ref, og_ref):
    tm = x_ref.shape[0]
    x = x_ref[...]
    xn = (x * lax.rsqrt(jnp.mean(x * x, axis=-1, keepdims=True) + NORM_EPS) * g_ref[...]).astype(jnp.bfloat16)
    xn_ref[...] = xn
    q = jnp.dot(xn, wq_ref[...], preferred_element_type=jnp.float32).astype(jnp.bfloat16)
    nt = (((1,), (1,)), ((), ()))
    for hc in range(2 * PEER_HEADS):
        sc_ref[hc] = lax.dot_general(keys_ref[hc], q[:, hc * PEER_SUBDIM:(hc + 1) * PEER_SUBDIM], nt,
                                     preferred_element_type=jnp.float32)

    def first_stage(hc, carry):
        vals, picks = _top16_rows(sc_ref[hc])
        sv_ref[hc] = vals
        si_ref[hc] = picks.astype(jnp.float32)
        return carry

    lax.fori_loop(0, 2 * PEER_HEADS, first_stage, 0)

    def second_stage(h, carry):
        best, ids = _best_sums(sv_ref[2 * h], sv_ref[2 * h + 1], si_ref[2 * h], si_ref[2 * h + 1])
        e = jnp.exp(best - best[0:1])
        r0 = pl.multiple_of(h * PEER_TOPK, PEER_TOPK)
        og_ref[pl.ds(r0, PEER_TOPK), :] = e / jnp.sum(e, axis=0, keepdims=True)
        oi_ref[pl.ds(r0, PEER_TOPK), :] = ids
        return carry

    lax.fori_loop(0, PEER_HEADS, second_stage, 0)
    idx_ref[...] = oi_ref[...].T.astype(jnp.int32)
    gate_ref[...] = og_ref[...].T


def peer_route(x, g, wq_bf, keys_bf, tm=512):
    t, d = x.shape
    nq = wq_bf.shape[1]
    hc = 2 * PEER_HEADS
    return pl.pallas_call(
        _peer_route_body,
        grid=(t // tm,),
        in_specs=[
            pl.BlockSpec((tm, d), lambda i: (i, 0)),
            pl.BlockSpec((1, d), lambda i: (0, 0)),
            pl.BlockSpec((d, nq), lambda i: (0, 0)),
            pl.BlockSpec((hc, PEER_NKEYS, PEER_SUBDIM), lambda i: (0, 0, 0)),
        ],
        out_specs=[pl.BlockSpec((tm, d), lambda i: (i, 0)),
                   pl.BlockSpec((tm, PAIRS), lambda i: (i, 0)),
                   pl.BlockSpec((tm, PAIRS), lambda i: (i, 0))],
        out_shape=[jax.ShapeDtypeStruct((t, d), jnp.bfloat16),
                   jax.ShapeDtypeStruct((t, PAIRS), jnp.int32),
                   jax.ShapeDtypeStruct((t, PAIRS), jnp.float32)],
        scratch_shapes=[pltpu.VMEM((hc, PEER_NKEYS, tm), jnp.float32),
                        pltpu.VMEM((hc, PEER_TOPK, tm), jnp.float32),
                        pltpu.VMEM((hc, PEER_TOPK, tm), jnp.float32),
                        pltpu.VMEM((PAIRS, tm), jnp.float32),
                        pltpu.VMEM((PAIRS, tm), jnp.float32)],
        compiler_params=pltpu.CompilerParams(
            dimension_semantics=("parallel",), vmem_limit_bytes=VMEM_LIMIT_BYTES),
        name="peer_route",
    )(x, g.reshape(1, d), wq_bf, keys_bf.reshape(hc, PEER_NKEYS, PEER_SUBDIM))


def peer_ffn(x, norm_g, wq, sub_keys, u_tab, v_tab):
    b, s_len, d = x.shape
    t = b * s_len
    x2 = x.reshape(t, d)
    xn_bf, idx, gate = peer_route(x2, norm_g, wq.astype(jnp.bfloat16), sub_keys.astype(jnp.bfloat16))
    wgt = peer_act(xn_bf, u_tab.astype(jnp.bfloat16), idx, gate)
    return peer_out(wgt, idx, v_tab.astype(jnp.bfloat16), x2).reshape(b, s_len, d)


def kernel(x, norm1_g, w_in, sb_qn_g, sb_kn_g, ml_conv_w, ml_conv_b, ml_i_b, ml_f_b, sg_vn_g, sg_w, sg_b,
           out_g, w_out, norm2_g, peer_wq, peer_keys, peer_u, peer_v):
    for l in range(DEPTH):
        x = mixing_sublayer(x, norm1_g[l], w_in[l], sb_qn_g[l], sb_kn_g[l], ml_conv_w[l], ml_conv_b[l],
                            ml_i_b[l], ml_f_b[l], sg_vn_g[l], sg_w[l], sg_b[l], out_g[l], w_out[l])
        x = peer_ffn(x, norm2_g[l], peer_wq[l], peer_keys[l], peer_u[l], peer_v[l])
    return x
```

```python
import functools

import jax
import jax.numpy as jnp
import numpy as np
from jax import lax
from jax.experimental import pallas as pl
from jax.experimental.pallas import tpu as pltpu

D_MODEL = 1024
DEPTH = 4
HEAD_DIM = 64
SB_HEADS = 6
ML_HEADS = 6
SG_GROUPS = 4
SB_WIDTH = SB_HEADS * HEAD_DIM
ML_WIDTH = ML_HEADS * HEAD_DIM
SG_WIDTH = SG_GROUPS * HEAD_DIM
MIX_HEADS = SB_HEADS + ML_HEADS + SG_GROUPS
MIX_WIDTH = SB_WIDTH + ML_WIDTH + SG_WIDTH
IN_SIZES = (SB_WIDTH, SB_WIDTH, SB_WIDTH, 2 * ML_WIDTH, ML_WIDTH, ML_WIDTH, ML_HEADS, ML_HEADS, 2 * SG_WIDTH)
IN_WIDTH = sum(IN_SIZES)
Q_BLOCK = 128
ML_CHUNK = 128
SG_CHUNK = 128
CONV_WIDTH = 4
PEER_HEADS = 8
PEER_NKEYS = 128
PEER_EXPERTS = PEER_NKEYS * PEER_NKEYS
PEER_SUBDIM = 128
PEER_TOPK = 16
PEER_BLOCK = 128
NORM_EPS = 1e-6

VMEM_LIMIT_BYTES = 56 * 1024 * 1024


LANES = 128
PAIRS = PEER_HEADS * PEER_TOPK
MXU_K = 256


def _peer_act_body(xn_ref, u_ref, idx_ref, gate_ref, o_ref, sel_ref, *, groups):
    ei = pl.program_id(1)

    @pl.when(ei == 0)
    def _():
        sel_ref[...] = jnp.zeros_like(sel_ref)

    idx = idx_ref[...]
    hi = idx >> 7
    lo = idx & (LANES - 1)
    sel = sel_ref[...]
    xn = xn_ref[...]
    for gg in range(groups * LANES // MXU_K):
        act = lax.dot_general(xn, u_ref[gg * MXU_K:(gg + 1) * MXU_K, :], (((1,), (1,)), ((), ())),
                              preferred_element_type=jnp.float32)
        for h in range(MXU_K // LANES):
            g = gg * (MXU_K // LANES) + h
            picked = jnp.take_along_axis(act[:, h * LANES:(h + 1) * LANES], lo, axis=1)
            sel = jnp.where(hi == ei * groups + g, picked, sel)
    sel_ref[...] = sel

    @pl.when(ei == pl.num_programs(1) - 1)
    def _():
        o_ref[...] = jax.nn.gelu(sel_ref[...]) * gate_ref[...]


def peer_act(xn_bf, u_bf, idx, gate, tm=512, te=4096):
    t, d = xn_bf.shape
    e = u_bf.shape[0]
    return pl.pallas_call(
        functools.partial(_peer_act_body, groups=te // LANES),
        grid=(t // tm, e // te),
        in_specs=[
            pl.BlockSpec((tm, d), lambda i, j: (i, 0)),
            pl.BlockSpec((te, d), lambda i, j: (j, 0)),
            pl.BlockSpec((tm, PAIRS), lambda i, j: (i, 0)),
            pl.BlockSpec((tm, PAIRS), lambda i, j: (i, 0)),
        ],
        out_specs=pl.BlockSpec((tm, PAIRS), lambda i, j: (i, 0)),
        out_shape=jax.ShapeDtypeStruct((t, PAIRS), jnp.float32),
        scratch_shapes=[pltpu.VMEM((tm, PAIRS), jnp.float32)],
        compiler_params=pltpu.CompilerParams(
            dimension_semantics=("parallel", "arbitrary"), vmem_limit_bytes=VMEM_LIMIT_BYTES),
        name="peer_act",
    )(xn_bf, u_bf, idx, gate)


SUBLANES = 8
SCATTER_TOKENS = 64
STAGE_PITCH = LANES + SUBLANES
REGROUP_UNROLL = 4


def _peer_out_body(wgt_ref, idx_ref, v_ref, x_ref, o_ref, w_ref, stage_ref, *, kgroups):
    ei = pl.program_id(1)
    tm = wgt_ref.shape[0]

    @pl.when(ei == 0)
    def _():
        sub_iota = lax.broadcasted_iota(jnp.int32, (LANES, PAIRS), 0).astype(jnp.float32).astype(jnp.bfloat16)
        one = jnp.ones((), jnp.bfloat16)
        zero = jnp.zeros((), jnp.bfloat16)

        def block(sb, carry):
            t0 = pl.multiple_of(sb * SCATTER_TOKENS, SCATTER_TOKENS)

            as_bf = lambda v: v.astype(jnp.float32).astype(jnp.bfloat16)
            for tb in range(SCATTER_TOKENS // SUBLANES):
                r0 = pl.multiple_of(t0 + tb * SUBLANES, SUBLANES)
                idx8 = idx_ref[pl.ds(r0, SUBLANES), :]
                hi8 = as_bf(idx8 >> 7)
                lo8 = as_bf(idx8 & (LANES - 1))
                w8 = wgt_ref[pl.ds(r0, SUBLANES), :].astype(jnp.bfloat16)
                for s in range(SUBLANES):
                    a = jnp.where(sub_iota == hi8[s:s + 1, :], one, zero)
                    b = jnp.where(sub_iota == lo8[s:s + 1, :], w8[s:s + 1, :], zero)
                    stage_ref[pl.ds((tb * SUBLANES + s) * STAGE_PITCH, LANES), :] = lax.dot_general(
                        a, b, (((1,), (1,)), ((), ())), preferred_element_type=jnp.float32)

            def regroup(ig, c2):
                for u in range(REGROUP_UNROLL):
                    ip = ig * REGROUP_UNROLL + u
                    for half in range(MXU_K // LANES):
                        rows = stage_ref[pl.ds(ip * (MXU_K // LANES) + half, SCATTER_TOKENS, stride=STAGE_PITCH), :]
                        w_ref[ip, pl.ds(t0, SCATTER_TOKENS), half * LANES:(half + 1) * LANES] = (
                            rows.astype(jnp.bfloat16))
                return c2

            lax.fori_loop(0, LANES * LANES // MXU_K // REGROUP_UNROLL, regroup, 0)
            return carry

        lax.fori_loop(0, tm // SCATTER_TOKENS, block, 0)
        o_ref[...] = x_ref[...]

    acc = o_ref[...]
    for g in range(kgroups):
        acc = acc + jnp.dot(w_ref[ei * kgroups + g], v_ref[g * MXU_K:(g + 1) * MXU_K, :],
                            preferred_element_type=jnp.float32)
    o_ref[...] = acc


def peer_out(wgt, idx, v_bf, x, tm=512, te=4096):
    t, d = x.shape
    e = v_bf.shape[0]
    return pl.pallas_call(
        functools.partial(_peer_out_body, kgroups=te // MXU_K),
        grid=(t // tm, e // te),
        in_specs=[
            pl.BlockSpec((tm, PAIRS), lambda i, j: (i, 0)),
            pl.BlockSpec((tm, PAIRS), lambda i, j: (i, 0)),
            pl.BlockSpec((te, d), lambda i, j: (j, 0)),
            pl.BlockSpec((tm, d), lambda i, j: (i, 0)),
        ],
        out_specs=pl.BlockSpec((tm, d), lambda i, j: (i, 0)),
        out_shape=jax.ShapeDtypeStruct((t, d), jnp.float32),
        scratch_shapes=[pltpu.VMEM((e // MXU_K, tm, MXU_K), jnp.bfloat16),
                        pltpu.VMEM((SCATTER_TOKENS * STAGE_PITCH, LANES), jnp.float32)],
        compiler_params=pltpu.CompilerParams(
            dimension_semantics=("parallel", "arbitrary"), vmem_limit_bytes=VMEM_LIMIT_BYTES),
        name="peer_out",
    )(wgt, idx, v_bf, x)


Z_BLOCKS = 25
Z_WIDTH = Z_BLOCKS * LANES
GATE_ROWS = 16


def _first_half(shape):
    return lax.broadcasted_iota(jnp.int32, shape, len(shape) - 1) < HEAD_DIM


def _pair_rms(x, g):
    first = _first_half(x.shape)
    x2 = x * x
    s_a = jnp.sum(jnp.where(first, x2, 0.0), axis=-1, keepdims=True)
    s_b = jnp.sum(jnp.where(first, 0.0, x2), axis=-1, keepdims=True)
    return x * lax.rsqrt(jnp.where(first, s_a, s_b) * (1.0 / HEAD_DIM) + NORM_EPS) * g


def _softplus(z):
    return jnp.maximum(z, 0.0) + jnp.log(1.0 + jnp.exp(-jnp.abs(z)))


def _in_proj_body(x_ref, g_ref, w_ref, wg_ref, z_ref, gt_ref):
    x = x_ref[...]
    xn = (x * lax.rsqrt(jnp.mean(x * x, axis=-1, keepdims=True) + NORM_EPS) * g_ref[...]).astype(jnp.bfloat16)
    z_ref[...] = jnp.dot(xn, w_ref[...], preferred_element_type=jnp.float32)
    gt_ref[...] = lax.dot_general(wg_ref[...], xn, (((1,), (1,)), ((), ())), preferred_element_type=jnp.float32)


def in_proj(x, g, w_main, w_gate_t, tm=512):
    t, d = x.shape
    return pl.pallas_call(
        _in_proj_body,
        grid=(t // tm,),
        in_specs=[
            pl.BlockSpec((tm, d), lambda i: (i, 0)),
            pl.BlockSpec((1, d), lambda i: (0, 0)),
            pl.BlockSpec((d, Z_WIDTH), lambda i: (0, 0)),
            pl.BlockSpec((GATE_ROWS, d), lambda i: (0, 0)),
        ],
        out_specs=[pl.BlockSpec((tm, Z_WIDTH), lambda i: (i, 0)),
                   pl.BlockSpec((GATE_ROWS, tm), lambda i: (0, i))],
        out_shape=[jax.ShapeDtypeStruct((t, Z_WIDTH), jnp.float32),
                   jax.ShapeDtypeStruct((GATE_ROWS, t), jnp.float32)],
        compiler_params=pltpu.CompilerParams(
            dimension_semantics=("parallel",), vmem_limit_bytes=VMEM_LIMIT_BYTES),
        name="in_proj",
    )(x, g.reshape(1, d), w_main, w_gate_t)


ATTN_BLOCK = 256
PREP_ROWS = 256
EXP_ZERO_LOG = -104.0


def _sb_attn_body(q_ref, k_ref, v_ref, qg_ref, kg_ref, og_ref, mcat_ref, o_ref, kn_ref, vb_ref):
    qi = pl.program_id(2)
    blk = ATTN_BLOCK
    s_len = k_ref.shape[1]

    @pl.when(qi == 0)
    def _():
        def prep(r, carry):
            r0 = pl.multiple_of(r * PREP_ROWS, PREP_ROWS)
            kn_ref[pl.ds(r0, PREP_ROWS), :] = _pair_rms(k_ref[0, pl.ds(r0, PREP_ROWS), :],
                                                        kg_ref[...]).astype(jnp.bfloat16)
            vb_ref[pl.ds(r0, PREP_ROWS), :] = v_ref[0, pl.ds(r0, PREP_ROWS), :].astype(jnp.bfloat16)
            return carry

        lax.fori_loop(0, s_len // PREP_ROWS, prep, 0)

    qn = _pair_rms(q_ref[0], qg_ref[...]) * (HEAD_DIM ** -0.5)
    first = _first_half(qn.shape)
    qh = jnp.concatenate([jnp.where(first, qn, 0.0), jnp.where(first, 0.0, qn)], axis=0).astype(jnp.bfloat16)
    strict = (lax.broadcasted_iota(jnp.int32, (2 * blk, blk), 1)
              < (lax.broadcasted_iota(jnp.int32, (2 * blk, blk), 0) & (blk - 1)))
    mcat = mcat_ref[...]
    nt = (((1,), (1,)), ((), ()))

    def key_block(j, c, acc, diagonal):
        k0 = pl.multiple_of(j * blk, blk)
        z = lax.dot_general(qh, kn_ref[pl.ds(k0, blk), :], nt, preferred_element_type=jnp.float32)
        lk = -_softplus(z)
        if diagonal:
            lk = jnp.where(strict, lk, 0.0)
        hi = lk.astype(jnp.bfloat16)
        lo = (lk - hi.astype(jnp.float32)).astype(jnp.bfloat16)
        ct = (jnp.dot(hi, mcat, preferred_element_type=jnp.float32)
              + jnp.dot(lo, mcat, preferred_element_type=jnp.float32))
        w = jnp.exp(z + c + ct[:, :blk])
        if diagonal:
            w = jnp.where(strict, w, 0.0)
        acc = acc + jnp.dot(w.astype(jnp.bfloat16), vb_ref[pl.ds(k0, blk), :], preferred_element_type=jnp.float32)
        return c + ct[:, blk:], acc

    c, acc = key_block(qi, jnp.zeros((2 * blk, blk), jnp.float32), jnp.zeros((2 * blk, LANES), jnp.float32), True)

    def cond(carry):
        j, go, _, _ = carry
        return jnp.logical_and(j >= 0, go > 0)

    def body(carry):
        j, _, c, acc = carry
        c, acc = key_block(j, c, acc, False)
        return j - 1, (jnp.max(c) > EXP_ZERO_LOG).astype(jnp.int32), c, acc

    _, _, _, acc = lax.while_loop(cond, body, (qi - 1, jnp.int32(1), c, acc))
    y = jnp.where(first, acc[:blk], acc[blk:])
    o_ref[0] = _pair_rms(y, og_ref[...]).astype(o_ref.dtype)


def _suffix_and_total_ones(n):
    r = np.arange(n)
    suffix = (r[:, None] >= r[None, :]).astype(np.float32)
    return jnp.asarray(np.concatenate([suffix, np.ones((n, n), np.float32)], axis=1), jnp.bfloat16)


def sb_attention(z3, qn_g, kn_g, out_g_row):
    b, s_len, _ = z3.shape
    pairs = SB_HEADS // 2
    blk = ATTN_BLOCK
    g2 = lambda g: jnp.concatenate([g, g]).reshape(1, LANES).astype(jnp.float32)
    return pl.pallas_call(
        _sb_attn_body,
        grid=(b, pairs, s_len // blk),
        in_specs=[
            pl.BlockSpec((1, blk, LANES), lambda bi, p, qi: (bi, qi, p)),
            pl.BlockSpec((1, s_len, LANES), lambda bi, p, qi: (bi, 0, pairs + p)),
            pl.BlockSpec((1, s_len, LANES), lambda bi, p, qi: (bi, 0, 2 * pairs + p)),
            pl.BlockSpec((1, LANES), lambda bi, p, qi: (0, 0)),
            pl.BlockSpec((1, LANES), lambda bi, p, qi: (0, 0)),
            pl.BlockSpec((1, LANES), lambda bi, p, qi: (0, p)),
            pl.BlockSpec((blk, 2 * blk), lambda bi, p, qi: (0, 0)),
        ],
        out_specs=pl.BlockSpec((1, blk, LANES), lambda bi, p, qi: (bi, qi, p)),
        out_shape=jax.ShapeDtypeStruct((b, s_len, SB_WIDTH), jnp.bfloat16),
        scratch_shapes=[pltpu.VMEM((s_len, LANES), jnp.bfloat16), pltpu.VMEM((s_len, LANES), jnp.bfloat16)],
        compiler_params=pltpu.CompilerParams(
            dimension_semantics=("parallel", "parallel", "arbitrary"), vmem_limit_bytes=VMEM_LIMIT_BYTES),
        name="sb_attention",
    )(z3, z3, z3, g2(qn_g), g2(kn_g), out_g_row, _suffix_and_total_ones(blk))


ML_SEQS = 2


def _lane_cumsum(x):
    lane = lax.broadcasted_iota(jnp.int32, x.shape, 1)
    shift = 1
    while shift < LANES:
        x = x + jnp.where(lane >= shift, pltpu.roll(x, shift, 1), 0.0)
        shift *= 2
    return x


def _as_column(row):
    return jnp.broadcast_to(row, (LANES, LANES)).T


def _mlstm_body(zq_ref, zk_ref, zv_ref, zo_ref, gt_ref, gb_ref, cwq_ref, cwk_ref, cbq_ref, cbk_ref, og_ref,
                o_ref, c_ref, n_ref, m_ref):
    s_len = zq_ref.shape[1]
    blk = ML_CHUNK
    c_ref[...] = jnp.zeros_like(c_ref)
    n_ref[...] = jnp.zeros_like(n_ref)
    m_ref[...] = jnp.zeros_like(m_ref)
    first = _first_half((blk, LANES))
    row = lax.broadcasted_iota(jnp.int32, (blk, blk), 0)
    col = lax.broadcasted_iota(jnp.int32, (blk, blk), 1)
    causal = col <= row
    same_head = (row >> 6) == (col >> 6)
    first_row = _first_half((1, LANES))
    nt = (((1,), (1,)), ((), ()))

    def chunk_of(sq, ci):
        t0 = pl.multiple_of(ci * blk, blk)
        tp = pl.multiple_of(jnp.maximum(t0 - SUBLANES, 0), SUBLANES)

        def conv_silu(z_ref, w_ref, b_ref):
            prev = jnp.where(ci > 0, z_ref[sq, pl.ds(tp, SUBLANES), :], 0.0)
            xx = jnp.concatenate([prev, z_ref[sq, pl.ds(t0, blk), :]], axis=0)
            out = b_ref[...]
            for j in range(CONV_WIDTH):
                lag = SUBLANES - (CONV_WIDTH - 1) + j
                out = out + xx[lag:lag + blk, :] * w_ref[j:j + 1, :]
            return out * jax.nn.sigmoid(out)

        q = conv_silu(zq_ref, cwq_ref, cbq_ref)
        k = conv_silu(zk_ref, cwk_ref, cbk_ref) * (HEAD_DIM ** -0.5)
        qb = q.astype(jnp.bfloat16)
        kb = k.astype(jnp.bfloat16)
        vb = zv_ref[sq, pl.ds(t0, blk), :].astype(jnp.bfloat16)

        g = gt_ref[sq * (s_len // blk) + ci, 0] + gb_ref[0]
        bcum = _lane_cumsum(-_softplus(-g))[2:4]
        ig = g[0:2]
        b_last = bcum[:, blk - 1:blk]
        gs = b_last - bcum + ig
        m_old = m_ref[sq, 0:2, 0:1]
        m_new = jnp.maximum(b_last + m_old, jnp.max(gs, axis=-1, keepdims=True))
        w_row = jnp.exp(gs - m_new)
        decay = jnp.exp(b_last + m_old - m_new)
        rowterm = ig - bcum

        nums, dens, mts, inters = [], [], [], []
        for h in range(2):
            qh = jnp.where(first if h == 0 else jnp.logical_not(first), q, 0.0).astype(jnp.bfloat16)
            qk = lax.dot_general(qh, kb, nt, preferred_element_type=jnp.float32)
            bcol = _as_column(bcum[h:h + 1])
            dmat = jnp.where(causal, bcol + rowterm[h:h + 1], -jnp.inf)
            m_int = bcol[:, 0:1] + m_old[h:h + 1]
            m_t = jnp.maximum(m_int, jnp.max(dmat, axis=-1, keepdims=True))
            pm = qk * jnp.exp(dmat - m_t)
            nums.append(jnp.dot(pm.astype(jnp.bfloat16), vb, preferred_element_type=jnp.float32))
            dens.append(jnp.sum(pm, axis=-1, keepdims=True))
            mts.append(m_t)
            inters.append(jnp.exp(m_int - m_t))

        pick = lambda ab: jnp.where(first, ab[0], ab[1])
        inter = pick(inters)
        qc = jnp.dot(qb, c_ref[sq].astype(jnp.bfloat16), preferred_element_type=jnp.float32)
        qn = q * n_ref[sq, 0:1, :]
        qn = jnp.where(first, jnp.sum(jnp.where(first, qn, 0.0), axis=-1, keepdims=True),
                       jnp.sum(jnp.where(first, 0.0, qn), axis=-1, keepdims=True))
        den = pick(dens) + inter * qn
        hout = (pick(nums) + inter * qc) / jnp.maximum(jnp.abs(den), jnp.exp(-pick(mts)))
        y = _pair_rms(hout, og_ref[...]) * jax.nn.sigmoid(zo_ref[sq, pl.ds(t0, blk), :])
        o_ref[sq, pl.ds(t0, blk), :] = y.astype(o_ref.dtype)

        kw = k * jnp.where(first, _as_column(w_row[0:1]), _as_column(w_row[1:2]))
        upd = jnp.dot(kw.T.astype(jnp.bfloat16), vb, preferred_element_type=jnp.float32)
        dec = jnp.where(first_row, decay[0:1], decay[1:2])
        c_ref[sq] = dec * c_ref[sq] + jnp.where(same_head, upd, 0.0)
        n_ref[sq, 0:1, :] = dec * n_ref[sq, 0:1, :] + jnp.sum(kw, axis=0, keepdims=True)
        m_ref[sq, 0:2, :] = jnp.broadcast_to(m_new, (2, LANES))

    def chunk(ci, carry):
        for sq in range(zq_ref.shape[0]):
            chunk_of(sq, ci)
        return carry

    lax.fori_loop(0, s_len // blk, chunk, 0)


def mlstm(z3, gates_t, ml_i_b, ml_f_b, conv_w, conv_b, out_g_row):
    b, s_len, _ = z3.shape
    pairs = ML_HEADS // 2
    blk = ML_CHUNK
    nc = s_len // blk
    gi = gates_t[:ML_HEADS].reshape(pairs, 2, b * nc, blk)
    gf = gates_t[ML_HEADS:2 * ML_HEADS].reshape(pairs, 2, b * nc, blk)
    gt = jnp.concatenate([gi, gf, jnp.zeros((pairs, SUBLANES - 4, b * nc, blk), jnp.float32)], axis=1)
    gt = gt.transpose(2, 0, 1, 3)
    gb = jnp.concatenate([ml_i_b.reshape(pairs, 2), ml_f_b.reshape(pairs, 2),
                          jnp.zeros((pairs, SUBLANES - 4), jnp.float32)], axis=1).reshape(pairs, SUBLANES, 1)
    nsq = ML_SEQS if b % ML_SEQS == 0 else 1
    zspec = lambda off: pl.BlockSpec((nsq, s_len, LANES), lambda bi, p: (bi, 0, off + p))
    return pl.pallas_call(
        _mlstm_body,
        grid=(b // nsq, pairs),
        in_specs=[
            zspec(9), zspec(12), zspec(15), zspec(18),
            pl.BlockSpec((nsq * nc, 1, SUBLANES, blk), lambda bi, p: (bi, p, 0, 0)),
            pl.BlockSpec((1, SUBLANES, 1), lambda bi, p: (p, 0, 0)),
            pl.BlockSpec((CONV_WIDTH, LANES), lambda bi, p: (0, p)),
            pl.BlockSpec((CONV_WIDTH, LANES), lambda bi, p: (0, pairs + p)),
            pl.BlockSpec((1, LANES), lambda bi, p: (0, p)),
            pl.BlockSpec((1, LANES), lambda bi, p: (0, pairs + p)),
            pl.BlockSpec((1, LANES), lambda bi, p: (0, SB_HEADS // 2 + p)),
        ],
        out_specs=pl.BlockSpec((nsq, s_len, LANES), lambda bi, p: (bi, 0, p)),
        out_shape=jax.ShapeDtypeStruct((b, s_len, ML_WIDTH), jnp.bfloat16),
        scratch_shapes=[pltpu.VMEM((nsq, LANES, LANES), jnp.float32),
                        pltpu.VMEM((nsq, SUBLANES, LANES), jnp.float32),
                        pltpu.VMEM((nsq, SUBLANES, LANES), jnp.float32)],
        compiler_params=pltpu.CompilerParams(
            dimension_semantics=("parallel", "parallel"), vmem_limit_bytes=VMEM_LIMIT_BYTES),
        name="mlstm",
    )(z3, z3, z3, z3, gt, gb, conv_w, conv_w, conv_b.reshape(1, -1), conv_b.reshape(1, -1), out_g_row)


SG_CHUNKS_PER_STEP = 8


def _sg_body(u_ref, v_ref, w_ref, bias_ref, vg_ref, og_ref, o_ref):
    blk = SG_CHUNK
    first = _first_half((blk, LANES))
    row = lax.broadcasted_iota(jnp.int32, (blk, blk), 0)
    col = lax.broadcasted_iota(jnp.int32, (blk, blk), 1)
    w_tril = [jnp.where(col <= row, w_ref[g], 0.0).astype(jnp.bfloat16) for g in range(2)]
    for ci in range(u_ref.shape[1] // blk):
        rows = slice(ci * blk, (ci + 1) * blk)
        u = jax.nn.gelu(u_ref[0, rows, :])
        v = _pair_rms(jax.nn.gelu(v_ref[0, rows, :]), vg_ref[...]).astype(jnp.bfloat16)
        gates = [jnp.dot(w_tril[g], v, preferred_element_type=jnp.float32) for g in range(2)]
        y = u * (jnp.where(first, gates[0], gates[1]) + bias_ref[0])
        o_ref[0, rows, :] = _pair_rms(y, og_ref[...]).astype(o_ref.dtype)


def spatial_gating(z3, sg_vn_g, sg_w, sg_b, out_g_row):
    b, s_len, _ = z3.shape
    pairs = SG_GROUPS // 2
    blk = SG_CHUNK
    bias = jnp.repeat(sg_b.astype(jnp.float32).reshape(pairs, 2, blk).transpose(0, 2, 1), HEAD_DIM, axis=2)
    rows = blk * SG_CHUNKS_PER_STEP if s_len % (blk * SG_CHUNKS_PER_STEP) == 0 else blk
    return pl.pallas_call(
        _sg_body,
        grid=(pairs, b, s_len // rows),
        in_specs=[
            pl.BlockSpec((1, rows, LANES), lambda p, bi, ci: (bi, ci, 21 + p)),
            pl.BlockSpec((1, rows, LANES), lambda p, bi, ci: (bi, ci, 23 + p)),
            pl.BlockSpec((2, blk, blk), lambda p, bi, ci: (p, 0, 0)),
            pl.BlockSpec((1, blk, LANES), lambda p, bi, ci: (p, 0, 0)),
            pl.BlockSpec((1, LANES), lambda p, bi, ci: (0, p)),
            pl.BlockSpec((1, LANES), lambda p, bi, ci: (0, (SB_HEADS + ML_HEADS) // 2 + p)),
        ],
        out_specs=pl.BlockSpec((1, rows, LANES), lambda p, bi, ci: (bi, ci, p)),
        out_shape=jax.ShapeDtypeStruct((b, s_len, SG_WIDTH), jnp.bfloat16),
        compiler_params=pltpu.CompilerParams(
            dimension_semantics=("parallel", "parallel", "parallel"), vmem_limit_bytes=VMEM_LIMIT_BYTES),
        name="spatial_gating",
    )(z3, z3, sg_w.astype(jnp.float32), bias, sg_vn_g.reshape(1, SG_WIDTH).astype(jnp.float32), out_g_row)


def _out_proj_body(x_ref, a_ref, b_ref, c_ref, w_ref, o_ref):
    acc = x_ref[...]
    acc = acc + jnp.dot(a_ref[...], w_ref[0:SB_WIDTH, :], preferred_element_type=jnp.float32)
    acc = acc + jnp.dot(b_ref[...], w_ref[SB_WIDTH:SB_WIDTH + ML_WIDTH, :], preferred_element_type=jnp.float32)
    acc = acc + jnp.dot(c_ref[...], w_ref[SB_WIDTH + ML_WIDTH:, :], preferred_element_type=jnp.float32)
    o_ref[...] = acc


def out_proj(x, y_sb, y_ml, y_sg, w_bf, tm=512):
    t, d = x.shape
    return pl.pallas_call(
        _out_proj_body,
        grid=(t // tm,),
        in_specs=[
            pl.BlockSpec((tm, d), lambda i: (i, 0)),
            pl.BlockSpec((tm, SB_WIDTH), lambda i: (i, 0)),
            pl.BlockSpec((tm, ML_WIDTH), lambda i: (i, 0)),
            pl.BlockSpec((tm, SG_WIDTH), lambda i: (i, 0)),
            pl.BlockSpec((MIX_WIDTH, d), lambda i: (0, 0)),
        ],
        out_specs=pl.BlockSpec((tm, d), lambda i: (i, 0)),
        out_shape=jax.ShapeDtypeStruct((t, d), jnp.float32),
        compiler_params=pltpu.CompilerParams(
            dimension_semantics=("parallel",), vmem_limit_bytes=VMEM_LIMIT_BYTES),
        name="out_proj",
    )(x, y_sb, y_ml, y_sg, w_bf)


def mixing_sublayer(x, norm_g, w_in, sb_qn_g, sb_kn_g, ml_conv_w, ml_conv_b, ml_i_b, ml_f_b,
                    sg_vn_g, sg_w, sg_b, out_g, w_out):
    b, s_len, d = x.shape
    t = b * s_len
    gate0 = sum(IN_SIZES[:6])
    gate1 = gate0 + 2 * ML_HEADS
    w_main = jnp.concatenate([w_in[:, :gate0], w_in[:, gate1:]], axis=1).astype(jnp.bfloat16)
    w_gate_t = jnp.pad(w_in[:, gate0:gate1].T, ((0, GATE_ROWS - 2 * ML_HEADS), (0, 0))).astype(jnp.bfloat16)
    z, gates_t = in_proj(x.reshape(t, d), norm_g, w_main, w_gate_t)
    z3 = z.reshape(b, s_len, Z_WIDTH)
    out_g_row = out_g.reshape(1, MIX_WIDTH).astype(jnp.float32)
    y_sb = sb_attention(z3, sb_qn_g, sb_kn_g, out_g_row)
    y_ml = mlstm(z3, gates_t, ml_i_b.astype(jnp.float32), ml_f_b.astype(jnp.float32),
                 ml_conv_w.astype(jnp.float32), ml_conv_b.astype(jnp.float32), out_g_row)
    y_sg = spatial_gating(z3, sg_vn_g, sg_w, sg_b, out_g_row)
    out = out_proj(x.reshape(t, d), y_sb.reshape(t, SB_WIDTH), y_ml.reshape(t, ML_WIDTH),
                   y_sg.reshape(t, SG_WIDTH), w_out.astype(jnp.bfloat16))
    return out.reshape(b, s_len, d)


NEG_INF = float("-inf")


def _top16_rows(s):
    n, tm = s.shape
    pos = lax.broadcasted_iota(jnp.int32, (n, tm), 0)
    slot = lax.broadcasted_iota(jnp.int32, (PEER_TOPK, tm), 0)
    vals = jnp.zeros((PEER_TOPK, tm), jnp.float32)
    picks = jnp.zeros((PEER_TOPK, tm), jnp.int32)
    for r in range(PEER_TOPK):
        m = jnp.max(s, axis=0, keepdims=True)
        at = jnp.min(jnp.where(s == m, pos, n), axis=0, keepdims=True)
        vals = jnp.where(slot == r, m, vals)
        picks = jnp.where(slot == r, at, picks)
        s = jnp.where(pos == at, NEG_INF, s)
    return vals, picks


def _best_sums(sv0, sv1, si0, si1):
    tm = sv0.shape[1]
    rows = lax.broadcasted_iota(jnp.int32, (PEER_TOPK, tm), 0)
    used = jnp.zeros((PEER_TOPK, tm), jnp.int32)
    front = sv0 + sv1[0:1]
    best = jnp.zeros((PEER_TOPK, tm), jnp.float32)
    pick_a = jnp.zeros((PEER_TOPK, tm), jnp.int32)
    pick_b = jnp.zeros((PEER_TOPK, tm), jnp.int32)
    for r in range(PEER_TOPK):
        m = jnp.max(front, axis=0, keepdims=True)
        a = jnp.min(jnp.where(front == m, rows, PEER_TOPK), axis=0, keepdims=True)
        hit = rows == a
        b = jnp.sum(jnp.where(hit, used, 0), axis=0, keepdims=True)
        best = jnp.where(rows == r, m, best)
        pick_a = jnp.where(rows == r, a, pick_a)
        pick_b = jnp.where(rows == r, b, pick_b)
        used = used + hit.astype(jnp.int32)
        nxt = jnp.sum(jnp.where(rows == b + 1, sv1, 0.0), axis=0, keepdims=True)
        front = jnp.where(hit, jnp.where(b + 1 < PEER_TOPK, sv0 + nxt, NEG_INF), front)
    hi = jnp.zeros((PEER_TOPK, tm), jnp.float32)
    lo = jnp.zeros((PEER_TOPK, tm), jnp.float32)
    for a in range(PEER_TOPK):
        hi = jnp.where(pick_a == a, si0[a:a + 1], hi)
        lo = jnp.where(pick_b == a, si1[a:a + 1], lo)
    return best, hi * PEER_NKEYS + lo


def _peer_route_body(x_ref, g_ref, wq_ref, keys_ref, xn_ref, idx_ref, gate_ref,
                     sc_ref, sv_ref, si_ref, oi_ref, og_ref):
    tm = x_ref.shape[0]
    x = x_ref[...]
    xn = (x * lax.rsqrt(jnp.mean(x * x, axis=-1, keepdims=True) + NORM_EPS) * g_ref[...]).astype(jnp.bfloat16)
    xn_ref[...] = xn
    q = jnp.dot(xn, wq_ref[...], preferred_element_type=jnp.float32).astype(jnp.bfloat16)
    nt = (((1,), (1,)), ((), ()))
    for hc in range(2 * PEER_HEADS):
        sc_ref[hc] = lax.dot_general(keys_ref[hc], q[:, hc * PEER_SUBDIM:(hc + 1) * PEER_SUBDIM], nt,
                                     preferred_element_type=jnp.float32)

    def first_stage(hc, carry):
        vals, picks = _top16_rows(sc_ref[hc])
        sv_ref[hc] = vals
        si_ref[hc] = picks.astype(jnp.float32)
        return carry

    lax.fori_loop(0, 2 * PEER_HEADS, first_stage, 0)

    def second_stage(h, carry):
        best, ids = _best_sums(sv_ref[2 * h], sv_ref[2 * h + 1], si_ref[2 * h], si_ref[2 * h + 1])
        e = jnp.exp(best - best[0:1])
        r0 = pl.multiple_of(h * PEER_TOPK, PEER_TOPK)
        og_ref[pl.ds(r0, PEER_TOPK), :] = e / jnp.sum(e, axis=0, keepdims=True)
        oi_ref[pl.ds(r0, PEER_TOPK), :] = ids
        return carry

    lax.fori_loop(0, PEER_HEADS, second_stage, 0)
    idx_ref[...] = oi_ref[...].T.astype(jnp.int32)
    gate_ref[...] = og_ref[...].T


def peer_route(x, g, wq_bf, keys_bf, tm=1024):
    t, d = x.shape
    nq = wq_bf.shape[1]
    hc = 2 * PEER_HEADS
    return pl.pallas_call(
        _peer_route_body,
        grid=(t // tm,),
        in_specs=[
            pl.BlockSpec((tm, d), lambda i: (i, 0)),
            pl.BlockSpec((1, d), lambda i: (0, 0)),
            pl.BlockSpec((d, nq), lambda i: (0, 0)),
            pl.BlockSpec((hc, PEER_NKEYS, PEER_SUBDIM), lambda i: (0, 0, 0)),
        ],
        out_specs=[pl.BlockSpec((tm, d), lambda i: (i, 0)),
                   pl.BlockSpec((tm, PAIRS), lambda i: (i, 0)),
                   pl.BlockSpec((tm, PAIRS), lambda i: (i, 0))],
        out_shape=[jax.ShapeDtypeStruct((t, d), jnp.bfloat16),
                   jax.ShapeDtypeStruct((t, PAIRS), jnp.int32),
                   jax.ShapeDtypeStruct((t, PAIRS), jnp.float32)],
        scratch_shapes=[pltpu.VMEM((hc, PEER_NKEYS, tm), jnp.float32),
                        pltpu.VMEM((hc, PEER_TOPK, tm), jnp.float32),
                        pltpu.VMEM((hc, PEER_TOPK, tm), jnp.float32),
                        pltpu.VMEM((PAIRS, tm), jnp.float32),
                        pltpu.VMEM((PAIRS, tm), jnp.float32)],
        compiler_params=pltpu.CompilerParams(
            dimension_semantics=("parallel",), vmem_limit_bytes=VMEM_LIMIT_BYTES),
        name="peer_route",
    )(x, g.reshape(1, d), wq_bf, keys_bf.reshape(hc, PEER_NKEYS, PEER_SUBDIM))


def peer_ffn(x, norm_g, wq, sub_keys, u_tab, v_tab):
    b, s_len, d = x.shape
    t = b * s_len
    x2 = x.reshape(t, d)
    xn_bf, idx, gate = peer_route(x2, norm_g, wq.astype(jnp.bfloat16), sub_keys.astype(jnp.bfloat16))
    wgt = peer_act(xn_bf, u_tab.astype(jnp.bfloat16), idx, gate)
    return peer_out(wgt, idx, v_tab.astype(jnp.bfloat16), x2).reshape(b, s_len, d)


def kernel(x, norm1_g, w_in, sb_qn_g, sb_kn_g, ml_conv_w, ml_conv_b, ml_i_b, ml_f_b, sg_vn_g, sg_w, sg_b,
           out_g, w_out, norm2_g, peer_wq, peer_keys, peer_u, peer_v):
    for l in range(DEPTH):
        x = mixing_sublayer(x, norm1_g[l], w_in[l], sb_qn_g[l], sb_kn_g[l], ml_conv_w[l], ml_conv_b[l],
                            ml_i_b[l], ml_f_b[l], sg_vn_g[l], sg_w[l], sg_b[l], out_g[l], w_out[l])
        x = peer_ffn(x, norm2_g[l], peer_wq[l], peer_keys[l], peer_u[l], peer_v[l])
    return x
```

```python
import functools

import jax
import jax.numpy as jnp
import numpy as np
from jax import lax
from jax.experimental import pallas as pl
from jax.experimental.pallas import tpu as pltpu

D_MODEL = 1024
DEPTH = 4
HEAD_DIM = 64
SB_HEADS = 6
ML_HEADS = 6
SG_GROUPS = 4
SB_WIDTH = SB_HEADS * HEAD_DIM
ML_WIDTH = ML_HEADS * HEAD_DIM
SG_WIDTH = SG_GROUPS * HEAD_DIM
MIX_HEADS = SB_HEADS + ML_HEADS + SG_GROUPS
MIX_WIDTH = SB_WIDTH + ML_WIDTH + SG_WIDTH
IN_SIZES = (SB_WIDTH, SB_WIDTH, SB_WIDTH, 2 * ML_WIDTH, ML_WIDTH, ML_WIDTH, ML_HEADS, ML_HEADS, 2 * SG_WIDTH)
IN_WIDTH = sum(IN_SIZES)
Q_BLOCK = 128
ML_CHUNK = 128
SG_CHUNK = 128
CONV_WIDTH = 4
PEER_HEADS = 8
PEER_NKEYS = 128
PEER_EXPERTS = PEER_NKEYS * PEER_NKEYS
PEER_SUBDIM = 128
PEER_TOPK = 16
PEER_BLOCK = 128
NORM_EPS = 1e-6

VMEM_LIMIT_BYTES = 56 * 1024 * 1024


LANES = 128
PAIRS = PEER_HEADS * PEER_TOPK
MXU_K = 256


def _peer_act_body(xn_ref, u_ref, idx_ref, gate_ref, o_ref, sel_ref, *, groups):
    ei = pl.program_id(1)

    @pl.when(ei == 0)
    def _():
        sel_ref[...] = jnp.zeros_like(sel_ref)

    idx = idx_ref[...]
    hi = idx >> 7
    lo = idx & (LANES - 1)
    sel = sel_ref[...]
    xn = xn_ref[...]
    for gg in range(groups * LANES // MXU_K):
        act = lax.dot_general(xn, u_ref[gg * MXU_K:(gg + 1) * MXU_K, :], (((1,), (1,)), ((), ())),
                              preferred_element_type=jnp.float32)
        for h in range(MXU_K // LANES):
            g = gg * (MXU_K // LANES) + h
            picked = jnp.take_along_axis(act[:, h * LANES:(h + 1) * LANES], lo, axis=1)
            sel = jnp.where(hi == ei * groups + g, picked, sel)
    sel_ref[...] = sel

    @pl.when(ei == pl.num_programs(1) - 1)
    def _():
        o_ref[...] = jax.nn.gelu(sel_ref[...]) * gate_ref[...]


def peer_act(xn_bf, u_bf, idx, gate, tm=512, te=8192):
    t, d = xn_bf.shape
    e = u_bf.shape[0]
    return pl.pallas_call(
        functools.partial(_peer_act_body, groups=te // LANES),
        grid=(t // tm, e // te),
        in_specs=[
            pl.BlockSpec((tm, d), lambda i, j: (i, 0)),
            pl.BlockSpec((te, d), lambda i, j: (j, 0)),
            pl.BlockSpec((tm, PAIRS), lambda i, j: (i, 0)),
            pl.BlockSpec((tm, PAIRS), lambda i, j: (i, 0)),
        ],
        out_specs=pl.BlockSpec((tm, PAIRS), lambda i, j: (i, 0)),
        out_shape=jax.ShapeDtypeStruct((t, PAIRS), jnp.float32),
        scratch_shapes=[pltpu.VMEM((tm, PAIRS), jnp.float32)],
        compiler_params=pltpu.CompilerParams(
            dimension_semantics=("parallel", "arbitrary"), vmem_limit_bytes=VMEM_LIMIT_BYTES),
        name="peer_act",
    )(xn_bf, u_bf, idx, gate)


SUBLANES = 8
SCATTER_TOKENS = 64
STAGE_PITCH = LANES + SUBLANES
REGROUP_UNROLL = 4


def _peer_out_body(wgt_ref, idx_ref, v_ref, x_ref, o_ref, w_ref, stage_ref, *, kgroups):
    ei = pl.program_id(1)
    tm = wgt_ref.shape[0]

    @pl.when(ei == 0)
    def _():
        sub_iota = lax.broadcasted_iota(jnp.int32, (LANES, PAIRS), 0).astype(jnp.float32).astype(jnp.bfloat16)
        one = jnp.ones((), jnp.bfloat16)
        zero = jnp.zeros((), jnp.bfloat16)

        def block(sb, carry):
            t0 = pl.multiple_of(sb * SCATTER_TOKENS, SCATTER_TOKENS)

            as_bf = lambda v: v.astype(jnp.float32).astype(jnp.bfloat16)
            for tb in range(SCATTER_TOKENS // SUBLANES):
                r0 = pl.multiple_of(t0 + tb * SUBLANES, SUBLANES)
                idx8 = idx_ref[pl.ds(r0, SUBLANES), :]
                hi8 = as_bf(idx8 >> 7)
                lo8 = as_bf(idx8 & (LANES - 1))
                w8 = wgt_ref[pl.ds(r0, SUBLANES), :].astype(jnp.bfloat16)
                for s in range(SUBLANES):
                    a = jnp.where(sub_iota == hi8[s:s + 1, :], one, zero)
                    b = jnp.where(sub_iota == lo8[s:s + 1, :], w8[s:s + 1, :], zero)
                    stage_ref[pl.ds((tb * SUBLANES + s) * STAGE_PITCH, LANES), :] = lax.dot_general(
                        a, b, (((1,), (1,)), ((), ())), preferred_element_type=jnp.float32)

            def regroup(ig, c2):
                for u in range(REGROUP_UNROLL):
                    ip = ig * REGROUP_UNROLL + u
                    for half in range(MXU_K // LANES):
                        rows = stage_ref[pl.ds(ip * (MXU_K // LANES) + half, SCATTER_TOKENS, stride=STAGE_PITCH), :]
                        w_ref[ip, pl.ds(t0, SCATTER_TOKENS), half * LANES:(half + 1) * LANES] = (
                            rows.astype(jnp.bfloat16))
                return c2

            lax.fori_loop(0, LANES * LANES // MXU_K // REGROUP_UNROLL, regroup, 0)
            return carry

        lax.fori_loop(0, tm // SCATTER_TOKENS, block, 0)
        o_ref[...] = x_ref[...]

    acc = o_ref[...]
    for g in range(kgroups):
        acc = acc + jnp.dot(w_ref[ei * kgroups + g], v_ref[g * MXU_K:(g + 1) * MXU_K, :],
                            preferred_element_type=jnp.float32)
    o_ref[...] = acc


def peer_out(wgt, idx, v_bf, x, tm=512, te=4096):
    t, d = x.shape
    e = v_bf.shape[0]
    return pl.pallas_call(
        functools.partial(_peer_out_body, kgroups=te // MXU_K),
        grid=(t // tm, e // te),
        in_specs=[
            pl.BlockSpec((tm, PAIRS), lambda i, j: (i, 0)),
            pl.BlockSpec((tm, PAIRS), lambda i, j: (i, 0)),
            pl.BlockSpec((te, d), lambda i, j: (j, 0)),
            pl.BlockSpec((tm, d), lambda i, j: (i, 0)),
        ],
        out_specs=pl.BlockSpec((tm, d), lambda i, j: (i, 0)),
        out_shape=jax.ShapeDtypeStruct((t, d), jnp.float32),
        scratch_shapes=[pltpu.VMEM((e // MXU_K, tm, MXU_K), jnp.bfloat16),
                        pltpu.VMEM((SCATTER_TOKENS * STAGE_PITCH, LANES), jnp.float32)],
        compiler_params=pltpu.CompilerParams(
            dimension_semantics=("parallel", "arbitrary"), vmem_limit_bytes=VMEM_LIMIT_BYTES),
        name="peer_out",
    )(wgt, idx, v_bf, x)


Z_BLOCKS = 25
Z_WIDTH = Z_BLOCKS * LANES
GATE_ROWS = 16


def _first_half(shape):
    return lax.broadcasted_iota(jnp.int32, shape, len(shape) - 1) < HEAD_DIM


def _pair_rms(x, g):
    first = _first_half(x.shape)
    x2 = x * x
    s_a = jnp.sum(jnp.where(first, x2, 0.0), axis=-1, keepdims=True)
    s_b = jnp.sum(jnp.where(first, 0.0, x2), axis=-1, keepdims=True)
    return x * lax.rsqrt(jnp.where(first, s_a, s_b) * (1.0 / HEAD_DIM) + NORM_EPS) * g


def _softplus(z):
    return jnp.maximum(z, 0.0) + jnp.log(1.0 + jnp.exp(-jnp.abs(z)))


def _in_proj_body(x_ref, g_ref, w_ref, wg_ref, z_ref, gt_ref):
    x = x_ref[...]
    xn = (x * lax.rsqrt(jnp.mean(x * x, axis=-1, keepdims=True) + NORM_EPS) * g_ref[...]).astype(jnp.bfloat16)
    z_ref[...] = jnp.dot(xn, w_ref[...], preferred_element_type=jnp.float32)
    gt_ref[...] = lax.dot_general(wg_ref[...], xn, (((1,), (1,)), ((), ())), preferred_element_type=jnp.float32)


def in_proj(x, g, w_main, w_gate_t, tm=512):
    t, d = x.shape
    return pl.pallas_call(
        _in_proj_body,
        grid=(t // tm,),
        in_specs=[
            pl.BlockSpec((tm, d), lambda i: (i, 0)),
            pl.BlockSpec((1, d), lambda i: (0, 0)),
            pl.BlockSpec((d, Z_WIDTH), lambda i: (0, 0)),
            pl.BlockSpec((GATE_ROWS, d), lambda i: (0, 0)),
        ],
        out_specs=[pl.BlockSpec((tm, Z_WIDTH), lambda i: (i, 0)),
                   pl.BlockSpec((GATE_ROWS, tm), lambda i: (0, i))],
        out_shape=[jax.ShapeDtypeStruct((t, Z_WIDTH), jnp.float32),
                   jax.ShapeDtypeStruct((GATE_ROWS, t), jnp.float32)],
        compiler_params=pltpu.CompilerParams(
            dimension_semantics=("parallel",), vmem_limit_bytes=VMEM_LIMIT_BYTES),
        name="in_proj",
    )(x, g.reshape(1, d), w_main, w_gate_t)


ATTN_BLOCK = 256
PREP_ROWS = 256
EXP_ZERO_LOG = -104.0


def _sb_attn_body(q_ref, k_ref, v_ref, qg_ref, kg_ref, og_ref, mcat_ref, o_ref, kn_ref, vb_ref):
    qi = pl.program_id(2)
    blk = ATTN_BLOCK
    s_len = k_ref.shape[1]

    @pl.when(qi == 0)
    def _():
        def prep(r, carry):
            r0 = pl.multiple_of(r * PREP_ROWS, PREP_ROWS)
            kn_ref[pl.ds(r0, PREP_ROWS), :] = _pair_rms(k_ref[0, pl.ds(r0, PREP_ROWS), :],
                                                        kg_ref[...]).astype(jnp.bfloat16)
            vb_ref[pl.ds(r0, PREP_ROWS), :] = v_ref[0, pl.ds(r0, PREP_ROWS), :].astype(jnp.bfloat16)
            return carry

        lax.fori_loop(0, s_len // PREP_ROWS, prep, 0)

    qn = _pair_rms(q_ref[0], qg_ref[...]) * (HEAD_DIM ** -0.5)
    first = _first_half(qn.shape)
    qh = jnp.concatenate([jnp.where(first, qn, 0.0), jnp.where(first, 0.0, qn)], axis=0).astype(jnp.bfloat16)
    strict = (lax.broadcasted_iota(jnp.int32, (2 * blk, blk), 1)
              < (lax.broadcasted_iota(jnp.int32, (2 * blk, blk), 0) & (blk - 1)))
    mcat = mcat_ref[...]
    nt = (((1,), (1,)), ((), ()))

    def key_block(j, c, acc, diagonal):
        k0 = pl.multiple_of(j * blk, blk)
        z = lax.dot_general(qh, kn_ref[pl.ds(k0, blk), :], nt, preferred_element_type=jnp.float32)
        lk = -_softplus(z)
        if diagonal:
            lk = jnp.where(strict, lk, 0.0)
        hi = lk.astype(jnp.bfloat16)
        lo = (lk - hi.astype(jnp.float32)).astype(jnp.bfloat16)
        ct = (jnp.dot(hi, mcat, preferred_element_type=jnp.float32)
              + jnp.dot(lo, mcat, preferred_element_type=jnp.float32))
        w = jnp.exp(z + c + ct[:, :blk])
        if diagonal:
            w = jnp.where(strict, w, 0.0)
        acc = acc + jnp.dot(w.astype(jnp.bfloat16), vb_ref[pl.ds(k0, blk), :], preferred_element_type=jnp.float32)
        return c + ct[:, blk:], acc

    c, acc = key_block(qi, jnp.zeros((2 * blk, blk), jnp.float32), jnp.zeros((2 * blk, LANES), jnp.float32), True)

    def cond(carry):
        j, go, _, _ = carry
        return jnp.logical_and(j >= 0, go > 0)

    def body(carry):
        j, _, c, acc = carry
        c, acc = key_block(j, c, acc, False)
        return j - 1, (jnp.max(c) > EXP_ZERO_LOG).astype(jnp.int32), c, acc

    _, _, _, acc = lax.while_loop(cond, body, (qi - 1, jnp.int32(1), c, acc))
    y = jnp.where(first, acc[:blk], acc[blk:])
    o_ref[0] = _pair_rms(y, og_ref[...]).astype(o_ref.dtype)


def _suffix_and_total_ones(n):
    r = np.arange(n)
    suffix = (r[:, None] >= r[None, :]).astype(np.float32)
    return jnp.asarray(np.concatenate([suffix, np.ones((n, n), np.float32)], axis=1), jnp.bfloat16)


def sb_attention(z3, qn_g, kn_g, out_g_row):
    b, s_len, _ = z3.shape
    pairs = SB_HEADS // 2
    blk = ATTN_BLOCK
    g2 = lambda g: jnp.concatenate([g, g]).reshape(1, LANES).astype(jnp.float32)
    return pl.pallas_call(
        _sb_attn_body,
        grid=(b, pairs, s_len // blk),
        in_specs=[
            pl.BlockSpec((1, blk, LANES), lambda bi, p, qi: (bi, qi, p)),
            pl.BlockSpec((1, s_len, LANES), lambda bi, p, qi: (bi, 0, pairs + p)),
            pl.BlockSpec((1, s_len, LANES), lambda bi, p, qi: (bi, 0, 2 * pairs + p)),
            pl.BlockSpec((1, LANES), lambda bi, p, qi: (0, 0)),
            pl.BlockSpec((1, LANES), lambda bi, p, qi: (0, 0)),
            pl.BlockSpec((1, LANES), lambda bi, p, qi: (0, p)),
            pl.BlockSpec((blk, 2 * blk), lambda bi, p, qi: (0, 0)),
        ],
        out_specs=pl.BlockSpec((1, blk, LANES), lambda bi, p, qi: (bi, qi, p)),
        out_shape=jax.ShapeDtypeStruct((b, s_len, SB_WIDTH), jnp.bfloat16),
        scratch_shapes=[pltpu.VMEM((s_len, LANES), jnp.bfloat16), pltpu.VMEM((s_len, LANES), jnp.bfloat16)],
        compiler_params=pltpu.CompilerParams(
            dimension_semantics=("parallel", "parallel", "arbitrary"), vmem_limit_bytes=VMEM_LIMIT_BYTES),
        name="sb_attention",
    )(z3, z3, z3, g2(qn_g), g2(kn_g), out_g_row, _suffix_and_total_ones(blk))


ML_SEQS = 2


def _lane_cumsum(x):
    lane = lax.broadcasted_iota(jnp.int32, x.shape, 1)
    shift = 1
    while shift < LANES:
        x = x + jnp.where(lane >= shift, pltpu.roll(x, shift, 1), 0.0)
        shift *= 2
    return x


def _as_column(row):
    return jnp.broadcast_to(row, (LANES, LANES)).T


def _mlstm_body(zq_ref, zk_ref, zv_ref, zo_ref, gt_ref, gb_ref, cwq_ref, cwk_ref, cbq_ref, cbk_ref, og_ref,
                o_ref, c_ref, n_ref, m_ref):
    s_len = zq_ref.shape[1]
    blk = ML_CHUNK
    c_ref[...] = jnp.zeros_like(c_ref)
    n_ref[...] = jnp.zeros_like(n_ref)
    m_ref[...] = jnp.zeros_like(m_ref)
    first = _first_half((blk, LANES))
    row = lax.broadcasted_iota(jnp.int32, (blk, blk), 0)
    col = lax.broadcasted_iota(jnp.int32, (blk, blk), 1)
    causal = col <= row
    same_head = (row >> 6) == (col >> 6)
    first_row = _first_half((1, LANES))
    nt = (((1,), (1,)), ((), ()))

    def chunk_of(sq, ci):
        t0 = pl.multiple_of(ci * blk, blk)
        tp = pl.multiple_of(jnp.maximum(t0 - SUBLANES, 0), SUBLANES)

        def conv_silu(z_ref, w_ref, b_ref):
            prev = jnp.where(ci > 0, z_ref[sq, pl.ds(tp, SUBLANES), :], 0.0)
            xx = jnp.concatenate([prev, z_ref[sq, pl.ds(t0, blk), :]], axis=0)
            out = b_ref[...]
            for j in range(CONV_WIDTH):
                lag = SUBLANES - (CONV_WIDTH - 1) + j
                out = out + xx[lag:lag + blk, :] * w_ref[j:j + 1, :]
            return out * jax.nn.sigmoid(out)

        q = conv_silu(zq_ref, cwq_ref, cbq_ref)
        k = conv_silu(zk_ref, cwk_ref, cbk_ref) * (HEAD_DIM ** -0.5)
        qb = q.astype(jnp.bfloat16)
        kb = k.astype(jnp.bfloat16)
        vb = zv_ref[sq, pl.ds(t0, blk), :].astype(jnp.bfloat16)

        g = gt_ref[sq * (s_len // blk) + ci, 0] + gb_ref[0]
        bcum = _lane_cumsum(-_softplus(-g))[2:4]
        ig = g[0:2]
        b_last = bcum[:, blk - 1:blk]
        gs = b_last - bcum + ig
        m_old = m_ref[sq, 0:2, 0:1]
        m_new = jnp.maximum(b_last + m_old, jnp.max(gs, axis=-1, keepdims=True))
        w_row = jnp.exp(gs - m_new)
        decay = jnp.exp(b_last + m_old - m_new)
        rowterm = ig - bcum

        nums, dens, mts, inters = [], [], [], []
        for h in range(2):
            qh = jnp.where(first if h == 0 else jnp.logical_not(first), q, 0.0).astype(jnp.bfloat16)
            qk = lax.dot_general(qh, kb, nt, preferred_element_type=jnp.float32)
            bcol = _as_column(bcum[h:h + 1])
            dmat = jnp.where(causal, bcol + rowterm[h:h + 1], -jnp.inf)
            m_int = bcol[:, 0:1] + m_old[h:h + 1]
            m_t = jnp.maximum(m_int, jnp.max(dmat, axis=-1, keepdims=True))
            pm = qk * jnp.exp(dmat - m_t)
            nums.append(jnp.dot(pm.astype(jnp.bfloat16), vb, preferred_element_type=jnp.float32))
            dens.append(jnp.sum(pm, axis=-1, keepdims=True))
            mts.append(m_t)
            inters.append(jnp.exp(m_int - m_t))

        pick = lambda ab: jnp.where(first, ab[0], ab[1])
        inter = pick(inters)
        qc = jnp.dot(qb, c_ref[sq].astype(jnp.bfloat16), preferred_element_type=jnp.float32)
        qn = q * n_ref[sq, 0:1, :]
        qn = jnp.where(first, jnp.sum(jnp.where(first, qn, 0.0), axis=-1, keepdims=True),
                       jnp.sum(jnp.where(first, 0.0, qn), axis=-1, keepdims=True))
        den = pick(dens) + inter * qn
        hout = (pick(nums) + inter * qc) / jnp.maximum(jnp.abs(den), jnp.exp(-pick(mts)))
        y = _pair_rms(hout, og_ref[...]) * jax.nn.sigmoid(zo_ref[sq, pl.ds(t0, blk), :])
        o_ref[sq, pl.ds(t0, blk), :] = y.astype(o_ref.dtype)

        kw = k * jnp.where(first, _as_column(w_row[0:1]), _as_column(w_row[1:2]))
        upd = jnp.dot(kw.T.astype(jnp.bfloat16), vb, preferred_element_type=jnp.float32)
        dec = jnp.where(first_row, decay[0:1], decay[1:2])
        c_ref[sq] = dec * c_ref[sq] + jnp.where(same_head, upd, 0.0)
        n_ref[sq, 0:1, :] = dec * n_ref[sq, 0:1, :] + jnp.sum(kw, axis=0, keepdims=True)
        m_ref[sq, 0:2, :] = jnp.broadcast_to(m_new, (2, LANES))

    def chunk(ci, carry):
        for sq in range(zq_ref.shape[0]):
            chunk_of(sq, ci)
        return carry

    lax.fori_loop(0, s_len // blk, chunk, 0)


def mlstm(z3, gates_t, ml_i_b, ml_f_b, conv_w, conv_b, out_g_row):
    b, s_len, _ = z3.shape
    pairs = ML_HEADS // 2
    blk = ML_CHUNK
    nc = s_len // blk
    gi = gates_t[:ML_HEADS].reshape(pairs, 2, b * nc, blk)
    gf = gates_t[ML_HEADS:2 * ML_HEADS].reshape(pairs, 2, b * nc, blk)
    gt = jnp.concatenate([gi, gf, jnp.zeros((pairs, SUBLANES - 4, b * nc, blk), jnp.float32)], axis=1)
    gt = gt.transpose(2, 0, 1, 3)
    gb = jnp.concatenate([ml_i_b.reshape(pairs, 2), ml_f_b.reshape(pairs, 2),
                          jnp.zeros((pairs, SUBLANES - 4), jnp.float32)], axis=1).reshape(pairs, SUBLANES, 1)
    nsq = ML_SEQS if b % ML_SEQS == 0 else 1
    zspec = lambda off: pl.BlockSpec((nsq, s_len, LANES), lambda bi, p: (bi, 0, off + p))
    return pl.pallas_call(
        _mlstm_body,
        grid=(b // nsq, pairs),
        in_specs=[
            zspec(9), zspec(12), zspec(15), zspec(18),
            pl.BlockSpec((nsq * nc, 1, SUBLANES, blk), lambda bi, p: (bi, p, 0, 0)),
            pl.BlockSpec((1, SUBLANES, 1), lambda bi, p: (p, 0, 0)),
            pl.BlockSpec((CONV_WIDTH, LANES), lambda bi, p: (0, p)),
            pl.BlockSpec((CONV_WIDTH, LANES), lambda bi, p: (0, pairs + p)),
            pl.BlockSpec((1, LANES), lambda bi, p: (0, p)),
            pl.BlockSpec((1, LANES), lambda bi, p: (0, pairs + p)),
            pl.BlockSpec((1, LANES), lambda bi, p: (0, SB_HEADS // 2 + p)),
        ],
        out_specs=pl.BlockSpec((nsq, s_len, LANES), lambda bi, p: (bi, 0, p)),
        out_shape=jax.ShapeDtypeStruct((b, s_len, ML_WIDTH), jnp.bfloat16),
        scratch_shapes=[pltpu.VMEM((nsq, LANES, LANES), jnp.float32),
                        pltpu.VMEM((nsq, SUBLANES, LANES), jnp.float32),
                        pltpu.VMEM((nsq, SUBLANES, LANES), jnp.float32)],
        compiler_params=pltpu.CompilerParams(
            dimension_semantics=("parallel", "parallel"), vmem_limit_bytes=VMEM_LIMIT_BYTES),
        name="mlstm",
    )(z3, z3, z3, z3, gt, gb, conv_w, conv_w, conv_b.reshape(1, -1), conv_b.reshape(1, -1), out_g_row)


SG_CHUNKS_PER_STEP = 8


def _sg_body(u_ref, v_ref, w_ref, bias_ref, vg_ref, og_ref, o_ref):
    blk = SG_CHUNK
    first = _first_half((blk, LANES))
    row = lax.broadcasted_iota(jnp.int32, (blk, blk), 0)
    col = lax.broadcasted_iota(jnp.int32, (blk, blk), 1)
    w_tril = [jnp.where(col <= row, w_ref[g], 0.0).astype(jnp.bfloat16) for g in range(2)]
    for ci in range(u_ref.shape[1] // blk):
        rows = slice(ci * blk, (ci + 1) * blk)
        u = jax.nn.gelu(u_ref[0, rows, :])
        v = _pair_rms(jax.nn.gelu(v_ref[0, rows, :]), vg_ref[...]).astype(jnp.bfloat16)
        gates = [jnp.dot(w_tril[g], v, preferred_element_type=jnp.float32) for g in range(2)]
        y = u * (jnp.where(first, gates[0], gates[1]) + bias_ref[0])
        o_ref[0, rows, :] = _pair_rms(y, og_ref[...]).astype(o_ref.dtype)


def spatial_gating(z3, sg_vn_g, sg_w, sg_b, out_g_row):
    b, s_len, _ = z3.shape
    pairs = SG_GROUPS // 2
    blk = SG_CHUNK
    bias = jnp.repeat(sg_b.astype(jnp.float32).reshape(pairs, 2, blk).transpose(0, 2, 1), HEAD_DIM, axis=2)
    rows = blk * SG_CHUNKS_PER_STEP if s_len % (blk * SG_CHUNKS_PER_STEP) == 0 else blk
    return pl.pallas_call(
        _sg_body,
        grid=(pairs, b, s_len // rows),
        in_specs=[
            pl.BlockSpec((1, rows, LANES), lambda p, bi, ci: (bi, ci, 21 + p)),
            pl.BlockSpec((1, rows, LANES), lambda p, bi, ci: (bi, ci, 23 + p)),
            pl.BlockSpec((2, blk, blk), lambda p, bi, ci: (p, 0, 0)),
            pl.BlockSpec((1, blk, LANES), lambda p, bi, ci: (p, 0, 0)),
            pl.BlockSpec((1, LANES), lambda p, bi, ci: (0, p)),
            pl.BlockSpec((1, LANES), lambda p, bi, ci: (0, (SB_HEADS + ML_HEADS) // 2 + p)),
        ],
        out_specs=pl.BlockSpec((1, rows, LANES), lambda p, bi, ci: (bi, ci, p)),
        out_shape=jax.ShapeDtypeStruct((b, s_len, SG_WIDTH), jnp.bfloat16),
        compiler_params=pltpu.CompilerParams(
            dimension_semantics=("parallel", "parallel", "parallel"), vmem_limit_bytes=VMEM_LIMIT_BYTES),
        name="spatial_gating",
    )(z3, z3, sg_w.astype(jnp.float32), bias, sg_vn_g.reshape(1, SG_WIDTH).astype(jnp.float32), out_g_row)


def _out_proj_body(x_ref, a_ref, b_ref, c_ref, w_ref, o_ref):
    acc = x_ref[...]
    acc = acc + jnp.dot(a_ref[...], w_ref[0:SB_WIDTH, :], preferred_element_type=jnp.float32)
    acc = acc + jnp.dot(b_ref[...], w_ref[SB_WIDTH:SB_WIDTH + ML_WIDTH, :], preferred_element_type=jnp.float32)
    acc = acc + jnp.dot(c_ref[...], w_ref[SB_WIDTH + ML_WIDTH:, :], preferred_element_type=jnp.float32)
    o_ref[...] = acc


def out_proj(x, y_sb, y_ml, y_sg, w_bf, tm=512):
    t, d = x.shape
    return pl.pallas_call(
        _out_proj_body,
        grid=(t // tm,),
        in_specs=[
            pl.BlockSpec((tm, d), lambda i: (i, 0)),
            pl.BlockSpec((tm, SB_WIDTH), lambda i: (i, 0)),
            pl.BlockSpec((tm, ML_WIDTH), lambda i: (i, 0)),
            pl.BlockSpec((tm, SG_WIDTH), lambda i: (i, 0)),
            pl.BlockSpec((MIX_WIDTH, d), lambda i: (0, 0)),
        ],
        out_specs=pl.BlockSpec((tm, d), lambda i: (i, 0)),
        out_shape=jax.ShapeDtypeStruct((t, d), jnp.float32),
        compiler_params=pltpu.CompilerParams(
            dimension_semantics=("parallel",), vmem_limit_bytes=VMEM_LIMIT_BYTES),
        name="out_proj",
    )(x, y_sb, y_ml, y_sg, w_bf)


def mixing_sublayer(x, norm_g, w_in, sb_qn_g, sb_kn_g, ml_conv_w, ml_conv_b, ml_i_b, ml_f_b,
                    sg_vn_g, sg_w, sg_b, out_g, w_out):
    b, s_len, d = x.shape
    t = b * s_len
    gate0 = sum(IN_SIZES[:6])
    gate1 = gate0 + 2 * ML_HEADS
    w_main = jnp.concatenate([w_in[:, :gate0], w_in[:, gate1:]], axis=1).astype(jnp.bfloat16)
    w_gate_t = jnp.pad(w_in[:, gate0:gate1].T, ((0, GATE_ROWS - 2 * ML_HEADS), (0, 0))).astype(jnp.bfloat16)
    z, gates_t = in_proj(x.reshape(t, d), norm_g, w_main, w_gate_t)
    z3 = z.reshape(b, s_len, Z_WIDTH)
    out_g_row = out_g.reshape(1, MIX_WIDTH).astype(jnp.float32)
    y_sb = sb_attention(z3, sb_qn_g, sb_kn_g, out_g_row)
    y_ml = mlstm(z3, gates_t, ml_i_b.astype(jnp.float32), ml_f_b.astype(jnp.float32),
                 ml_conv_w.astype(jnp.float32), ml_conv_b.astype(jnp.float32), out_g_row)
    y_sg = spatial_gating(z3, sg_vn_g, sg_w, sg_b, out_g_row)
    out = out_proj(x.reshape(t, d), y_sb.reshape(t, SB_WIDTH), y_ml.reshape(t, ML_WIDTH),
                   y_sg.reshape(t, SG_WIDTH), w_out.astype(jnp.bfloat16))
    return out.reshape(b, s_len, d)


NEG_INF = float("-inf")


def _top16_rows(s):
    n, tm = s.shape
    pos = lax.broadcasted_iota(jnp.int32, (n, tm), 0)
    slot = lax.broadcasted_iota(jnp.int32, (PEER_TOPK, tm), 0)
    vals = jnp.zeros((PEER_TOPK, tm), jnp.float32)
    picks = jnp.zeros((PEER_TOPK, tm), jnp.int32)
    for r in range(PEER_TOPK):
        m = jnp.max(s, axis=0, keepdims=True)
        at = jnp.min(jnp.where(s == m, pos, n), axis=0, keepdims=True)
        vals = jnp.where(slot == r, m, vals)
        picks = jnp.where(slot == r, at, picks)
        s = jnp.where(pos == at, NEG_INF, s)
    return vals, picks


def _best_sums(sv0, sv1, si0, si1):
    tm = sv0.shape[1]
    rows = lax.broadcasted_iota(jnp.int32, (PEER_TOPK, tm), 0)
    used = jnp.zeros((PEER_TOPK, tm), jnp.int32)
    front = sv0 + sv1[0:1]
    best = jnp.zeros((PEER_TOPK, tm), jnp.float32)
    pick_a = jnp.zeros((PEER_TOPK, tm), jnp.int32)
    pick_b = jnp.zeros((PEER_TOPK, tm), jnp.int32)
    for r in range(PEER_TOPK):
        m = jnp.max(front, axis=0, keepdims=True)
        a = jnp.min(jnp.where(front == m, rows, PEER_TOPK), axis=0, keepdims=True)
        hit = rows == a
        b = jnp.sum(jnp.where(hit, used, 0), axis=0, keepdims=True)
        best = jnp.where(rows == r, m, best)
        pick_a = jnp.where(rows == r, a, pick_a)
        pick_b = jnp.where(rows == r, b, pick_b)
        used = used + hit.astype(jnp.int32)
        nxt = jnp.sum(jnp.where(rows == b + 1, sv1, 0.0), axis=0, keepdims=True)
        front = jnp.where(hit, jnp.where(b + 1 < PEER_TOPK, sv0 + nxt, NEG_INF), front)
    hi = jnp.zeros((PEER_TOPK, tm), jnp.float32)
    lo = jnp.zeros((PEER_TOPK, tm), jnp.float32)
    for a in range(PEER_TOPK):
        hi = jnp.where(pick_a == a, si0[a:a + 1], hi)
        lo = jnp.where(pick_b == a, si1[a:a + 1], lo)
    return best, hi * PEER_NKEYS + lo


def _peer_route_body(x_ref, g_ref, wq_ref, keys_ref, xn_ref, idx_ref, gate_ref,
                     sc_ref, sv_ref, si_ref, oi_ref, og_ref):
    tm = x_ref.shape[0]
    x = x_ref[...]
    xn = (x * lax.rsqrt(jnp.mean(x * x, axis=-1, keepdims=True) + NORM_EPS) * g_ref[...]).astype(jnp.bfloat16)
    xn_ref[...] = xn
    q = jnp.dot(xn, wq_ref[...], preferred_element_type=jnp.float32).astype(jnp.bfloat16)
    nt = (((1,), (1,)), ((), ()))
    for hc in range(2 * PEER_HEADS):
        sc_ref[hc] = lax.dot_general(keys_ref[hc], q[:, hc * PEER_SUBDIM:(hc + 1) * PEER_SUBDIM], nt,
                                     preferred_element_type=jnp.float32)

    def first_stage(hc, carry):
        vals, picks = _top16_rows(sc_ref[hc])
        sv_ref[hc] = vals
        si_ref[hc] = picks.astype(jnp.float32)
        return carry

    lax.fori_loop(0, 2 * PEER_HEADS, first_stage, 0)

    def second_stage(h, carry):
        best, ids = _best_sums(sv_ref[2 * h], sv_ref[2 * h + 1], si_ref[2 * h], si_ref[2 * h + 1])
        e = jnp.exp(best - best[0:1])
        r0 = pl.multiple_of(h * PEER_TOPK, PEER_TOPK)
        og_ref[pl.ds(r0, PEER_TOPK), :] = e / jnp.sum(e, axis=0, keepdims=True)
        oi_ref[pl.ds(r0, PEER_TOPK), :] = ids
        return carry

    lax.fori_loop(0, PEER_HEADS, second_stage, 0)
    idx_ref[...] = oi_ref[...].T.astype(jnp.int32)
    gate_ref[...] = og_ref[...].T


def peer_route(x, g, wq_bf, keys_bf, tm=1024):
    t, d = x.shape
    nq = wq_bf.shape[1]
    hc = 2 * PEER_HEADS
    return pl.pallas_call(
        _peer_route_body,
        grid=(t // tm,),
        in_specs=[
            pl.BlockSpec((tm, d), lambda i: (i, 0)),
            pl.BlockSpec((1, d), lambda i: (0, 0)),
            pl.BlockSpec((d, nq), lambda i: (0, 0)),
            pl.BlockSpec((hc, PEER_NKEYS, PEER_SUBDIM), lambda i: (0, 0, 0)),
        ],
        out_specs=[pl.BlockSpec((tm, d), lambda i: (i, 0)),
                   pl.BlockSpec((tm, PAIRS), lambda i: (i, 0)),
                   pl.BlockSpec((tm, PAIRS), lambda i: (i, 0))],
        out_shape=[jax.ShapeDtypeStruct((t, d), jnp.bfloat16),
                   jax.ShapeDtypeStruct((t, PAIRS), jnp.int32),
                   jax.ShapeDtypeStruct((t, PAIRS), jnp.float32)],
        scratch_shapes=[pltpu.VMEM((hc, PEER_NKEYS, tm), jnp.float32),
                        pltpu.VMEM((hc, PEER_TOPK, tm), jnp.float32),
                        pltpu.VMEM((hc, PEER_TOPK, tm), jnp.float32),
                        pltpu.VMEM((PAIRS, tm), jnp.float32),
                        pltpu.VMEM((PAIRS, tm), jnp.float32)],
        compiler_params=pltpu.CompilerParams(
            dimension_semantics=("parallel",), vmem_limit_bytes=VMEM_LIMIT_BYTES),
        name="peer_route",
    )(x, g.reshape(1, d), wq_bf, keys_bf.reshape(hc, PEER_NKEYS, PEER_SUBDIM))


def peer_ffn(x, norm_g, wq, sub_keys, u_tab, v_tab):
    b, s_len, d = x.shape
    t = b * s_len
    x2 = x.reshape(t, d)
    xn_bf, idx, gate = peer_route(x2, norm_g, wq.astype(jnp.bfloat16), sub_keys.astype(jnp.bfloat16))
    wgt = peer_act(xn_bf, u_tab.astype(jnp.bfloat16), idx, gate)
    return peer_out(wgt, idx, v_tab.astype(jnp.bfloat16), x2).reshape(b, s_len, d)


def kernel(x, norm1_g, w_in, sb_qn_g, sb_kn_g, ml_conv_w, ml_conv_b, ml_i_b, ml_f_b, sg_vn_g, sg_w, sg_b,
           out_g, w_out, norm2_g, peer_wq, peer_keys, peer_u, peer_v):
    for l in range(DEPTH):
        x = mixing_sublayer(x, norm1_g[l], w_in[l], sb_qn_g[l], sb_kn_g[l], ml_conv_w[l], ml_conv_b[l],
                            ml_i_b[l], ml_f_b[l], sg_vn_g[l], sg_w[l], sg_b[l], out_g[l], w_out[l])
        x = peer_ffn(x, norm2_g[l], peer_wq[l], peer_keys[l], peer_u[l], peer_v[l])
    return x
```
